```python
import math
import jax
import jax.numpy as jnp
from jax import lax
import numpy as np

D_MODEL = 2048
BATCH = 4
SEQ = 4096
DEPTH = 4

MIX_WIDTH = D_MODEL // 2
POOL_WINDOWS = (2, 4, 8, 16)
POOL_GROUPS = len(POOL_WINDOWS)
POOL_GROUP_DIM = MIX_WIDTH // POOL_GROUPS
DIFF_HEADS = 8
DIFF_QK_DIM = MIX_WIDTH // (2 * DIFF_HEADS)
DIFF_V_DIM = MIX_WIDTH // DIFF_HEADS
Q_BLOCK = 128
RET_HEADS = 8
RET_K_DIM = MIX_WIDTH // RET_HEADS
RET_V_DIM = MIX_WIDTH // RET_HEADS
RET_CHUNK = 128
ROPE_BASE = 10000.0
N_BRANCHES = 3
D_FF = 4 * D_MODEL
REL_BUCKETS = 32
REL_MAX_DIST = 128
NORM_EPS = 1e-6
NEG_INF = -1e30
IN_SPLITS = (MIX_WIDTH,) * 8 + (N_BRANCHES * D_MODEL,)
IN_COLS = sum(IN_SPLITS)

kernel_name = 'hybrid_gated_pool_diffattn_retention_block'


def rmsnorm(x, gain=None):
    xf = x.astype(jnp.float32)
    y = xf * lax.rsqrt(jnp.mean(xf * xf, axis=-1, keepdims=True) + NORM_EPS)
    if gain is not None:
        y = y * gain.astype(jnp.float32)
    return y.astype(x.dtype)


def t5_causal_bucket(q_pos, k_pos):
    n = jnp.maximum(q_pos[:, None] - k_pos[None, :], 0)
    exact = REL_BUCKETS // 2
    nf = jnp.maximum(n, 1).astype(jnp.float32)
    large = exact + (jnp.log(nf / exact) / math.log(REL_MAX_DIST / exact)
                     * (REL_BUCKETS - exact)).astype(jnp.int32)
    large = jnp.minimum(large, REL_BUCKETS - 1)
    return jnp.where(n < exact, n, large)


def rotary(t):
    S, half = t.shape[1], t.shape[-1] // 2
    inv_freq = 1.0 / (ROPE_BASE ** jnp.linspace(0.0, 1.0, half, dtype=jnp.float32))
    ang = jnp.arange(S, dtype=jnp.float32)[:, None] * inv_freq[None, :]
    cos = jnp.cos(ang)[None, :, None, :]
    sin = jnp.sin(ang)[None, :, None, :]
    t1, t2 = t[..., :half], t[..., half:]
    return jnp.concatenate([t1 * cos - t2 * sin, t2 * cos + t1 * sin], axis=-1)


def pool_mixer(u, pool_w, pool_scale):
    B, S, _ = u.shape
    uf = u.astype(jnp.float32).reshape(B, S, POOL_GROUPS, POOL_GROUP_DIM)
    cs = jnp.pad(jnp.cumsum(uf, axis=1), ((0, 0), (1, 0), (0, 0), (0, 0)))
    t = jnp.arange(S)
    win = jnp.array(POOL_WINDOWS, jnp.int32)
    lo = jnp.maximum(t[:, None] + 1 - win[None, :], 0)
    g_idx = jnp.arange(POOL_GROUPS)[None, :]
    lower = cs[:, lo, g_idx, :]
    count = (t[:, None] + 1 - lo).astype(jnp.float32)
    delta = (cs[:, 1:] - lower) / count[None, :, :, None] - uf
    y = jnp.einsum('bsgc,gcd->bsgd', delta.astype(u.dtype), pool_w)
    return y.reshape(B, S, MIX_WIDTH) * pool_scale


def diff_attention(q, k, v, lam_params, head_gain, rel_bias, lambda_init):
    B, S, _ = q.shape
    nb = S // Q_BLOCK
    q = q.reshape(B, S, DIFF_HEADS, 2, DIFF_QK_DIM).transpose(0, 2, 3, 1, 4)
    k = k.reshape(B, S, DIFF_HEADS, 2, DIFF_QK_DIM).transpose(0, 2, 3, 1, 4)
    v = v.reshape(B, S, DIFF_HEADS, DIFF_V_DIM).transpose(0, 2, 1, 3)
    lp = lam_params.astype(jnp.float32)
    lam = jnp.exp(jnp.sum(lp[0] * lp[1])) - jnp.exp(jnp.sum(lp[2] * lp[3])) + lambda_init
    scale = DIFF_QK_DIM ** -0.5
    k_pos = jnp.arange(S)
    bias_table = rel_bias.astype(jnp.float32)
    q_blocks = q.reshape(B, DIFF_HEADS, 2, nb, Q_BLOCK, DIFF_QK_DIM).transpose(3, 0, 1, 2, 4, 5)

    def block(args):
        qb, blk = args
        q_pos = blk * Q_BLOCK + jnp.arange(Q_BLOCK)
        bias = bias_table[t5_causal_bucket(q_pos, k_pos)].transpose(2, 0, 1)
        logits = jnp.einsum('bhmqd,bhmkd->bhmqk', qb, k).astype(jnp.float32) * scale
        logits = logits + bias[None, :, None]
        causal = k_pos[None, :] <= q_pos[:, None]
        p = jax.nn.softmax(jnp.where(causal, logits, NEG_INF), axis=-1)
        attn = p[:, :, 0] - lam * p[:, :, 1]
        return jnp.einsum('bhqk,bhkd->bhqd', attn.astype(v.dtype), v)

    out = lax.map(block, (q_blocks, jnp.arange(nb)))
    out = out.transpose(1, 2, 0, 3, 4).reshape(B, DIFF_HEADS, S, DIFF_V_DIM)
    out = rmsnorm(out, head_gain) * (1.0 - lambda_init)
    return out.transpose(0, 2, 1, 3).reshape(B, S, MIX_WIDTH)


def retention(q, k, v, g):
    B, S, _ = q.shape
    nc = S // RET_CHUNK
    f32 = jnp.float32
    qh = rotary(q.astype(f32).reshape(B, S, RET_HEADS, RET_K_DIM))
    kh = rotary(k.astype(f32).reshape(B, S, RET_HEADS, RET_K_DIM)) * RET_K_DIM ** -0.5
    vh = v.astype(f32).reshape(B, S, RET_HEADS, RET_V_DIM)

    def chunks(t):
        return t.reshape(B, nc, RET_CHUNK, RET_HEADS, t.shape[-1]).transpose(1, 0, 3, 2, 4)

    log_gamma = jnp.log(1.0 - 2.0 ** (-5.0 - jnp.arange(RET_HEADS, dtype=f32)))
    pos = jnp.arange(RET_CHUNK, dtype=f32)
    rel = pos[:, None] - pos[None, :]
    intra = jnp.where(rel >= 0, jnp.exp(jnp.maximum(rel, 0.0) * log_gamma[:, None, None]), 0.0)
    key_decay = jnp.exp((RET_CHUNK - 1 - pos)[None, :] * log_gamma[:, None])
    query_decay = jnp.exp((pos + 1)[None, :] * log_gamma[:, None])
    chunk_decay = jnp.exp(RET_CHUNK * log_gamma)

    def step(state, qkv):
        qc, kc, vc = qkv
        scores = jnp.einsum('bhid,bhjd->bhij', qc, kc) * intra
        inner = jnp.einsum('bhij,bhje->bhie', scores, vc)
        cross = jnp.einsum('bhid,bhde->bhie', qc, state) * query_decay[..., None]
        state = chunk_decay[:, None, None] * state + jnp.einsum(
            'bhjd,bhje->bhde', kc * key_decay[..., None], vc)
        return state, inner + cross

    state0 = jnp.zeros((B, RET_HEADS, RET_K_DIM, RET_V_DIM), f32)
    _, out = lax.scan(step, state0, (chunks(qh), chunks(kh), chunks(vh)))
    out = out.transpose(1, 0, 3, 2, 4).reshape(B, S, RET_HEADS, RET_V_DIM)
    out = rmsnorm(out).reshape(B, S, MIX_WIDTH)
    return (jax.nn.silu(g.astype(f32)) * out).astype(g.dtype)


def setup_inputs(seed: int = 0) -> dict:
    key = jax.random.key(seed)
    ks = jax.random.split(key, 15)
    f32 = jnp.float32

    def normal(k, shape, scale):
        return jax.random.normal(k, shape, f32) * scale

    def gain(k, shape):
        return 1.0 + 0.02 * jax.random.normal(k, shape, f32)

    return {
        'x': normal(ks[0], (BATCH, SEQ, D_MODEL), 1.0),
        'rel_bias': normal(ks[1], (REL_BUCKETS, DIFF_HEADS), 0.5),
        'norm_pre_mix': gain(ks[2], (DEPTH, D_MODEL)),
        'norm_post_mix': gain(ks[3], (DEPTH, D_MODEL)),
        'norm_pre_mlp': gain(ks[4], (DEPTH, D_MODEL)),
        'norm_post_mlp': gain(ks[5], (DEPTH, D_MODEL)),
        'w_in': normal(ks[6], (DEPTH, D_MODEL, IN_COLS), D_MODEL ** -0.5),
        'pool_w': normal(ks[7], (DEPTH, POOL_GROUPS, POOL_GROUP_DIM, POOL_GROUP_DIM), POOL_GROUP_DIM ** -0.5),
        'pool_scale': gain(ks[8], (DEPTH, MIX_WIDTH)),
        'diff_lambda': normal(ks[9], (DEPTH, 4, DIFF_QK_DIM), 0.1),
        'diff_head_norm': gain(ks[10], (DEPTH, DIFF_V_DIM)),
        'w_branch': normal(ks[11], (DEPTH, N_BRANCHES, MIX_WIDTH, D_MODEL), MIX_WIDTH ** -0.5),
        'w_out': normal(ks[12], (DEPTH, D_MODEL, D_MODEL), D_MODEL ** -0.5),
        'w_up': normal(ks[13], (DEPTH, D_MODEL, D_FF), D_MODEL ** -0.5),
        'w_down': normal(ks[14], (DEPTH, D_FF, D_MODEL), D_FF ** -0.5),
    }


def reference(x, rel_bias, norm_pre_mix, norm_post_mix, norm_pre_mlp, norm_post_mlp,
              w_in, pool_w, pool_scale, diff_lambda, diff_head_norm, w_branch, w_out,
              w_up, w_down):
    B, S, _ = x.shape
    split_at = [int(i) for i in np.cumsum(IN_SPLITS)[:-1]]
    for l in range(DEPTH):
        lambda_init = 0.8 - 0.6 * math.exp(-0.3 * l)
        a = rmsnorm(x, norm_pre_mix[l])
        proj = jnp.einsum('bsd,dc->bsc', a, w_in[l])
        u_pool, q_d, k_d, v_d, q_r, k_r, v_r, g_r, gate_logits = jnp.split(proj, split_at, axis=-1)
        y_pool = pool_mixer(u_pool, pool_w[l], pool_scale[l])
        y_diff = diff_attention(q_d, k_d, v_d, diff_lambda[l], diff_head_norm[l], rel_bias, lambda_init)
        y_ret = retention(q_r, k_r, v_r, g_r)
        branches = jnp.stack([y_pool, y_diff, y_ret], axis=2)
        widened = jnp.einsum('bsnc,ncd->bsnd', branches, w_branch[l])
        gates = jax.nn.sigmoid(gate_logits.reshape(B, S, N_BRANCHES, D_MODEL))
        merged = jnp.sum(gates * widened, axis=2)
        mix = jnp.einsum('bsd,de->bse', merged, w_out[l])
        x = x + rmsnorm(mix, norm_post_mix[l])
        h = rmsnorm(x, norm_pre_mlp[l])
        ff = jnp.square(jax.nn.relu(jnp.einsum('bsd,df->bsf', h, w_up[l])))
        ff = jnp.einsum('bsf,fd->bsd', ff, w_down[l])
        x = x + rmsnorm(ff, norm_post_mlp[l])
    return x
```

```python
import functools
import math

import jax
import jax.numpy as jnp
from jax import lax
from jax.experimental import pallas as pl
from jax.experimental.pallas import tpu as pltpu

F32 = jnp.float32
BF16 = jnp.bfloat16

LANES = 128
POOL_WINDOWS = (2, 4, 8, 16)
POOL_HALO = 16
DIFF_HEADS = 8
DIFF_QK_DIM = 64
DIFF_V_DIM = 128
RET_HEADS = 8
RET_DIM = 128
RET_CHUNK = 128
ROPE_BASE = 10000.0
N_BRANCHES = 3
REL_BUCKETS = 32
REL_MAX_DIST = 128
NORM_EPS = 1e-6
NEG_INF = -1e30
VMEM_LIMIT = 56 * 1024 * 1024


def _params(*semantics):
    return pltpu.CompilerParams(dimension_semantics=semantics, vmem_limit_bytes=VMEM_LIMIT)


def _rmsnorm_rows(x, gain=None):
    y = x * lax.rsqrt(jnp.mean(x * x, axis=-1, keepdims=True) + NORM_EPS)
    return y if gain is None else y * gain


def _inproj_kernel(x_ref, g_ref, w_ref, o_ref, a_scr):
    @pl.when(pl.program_id(1) == 0)
    def _():
        a_scr[...] = _rmsnorm_rows(x_ref[...], g_ref[...]).astype(BF16)

    o_ref[...] = jnp.dot(a_scr[...], w_ref[...], preferred_element_type=F32).astype(o_ref.dtype)


def _inproj(x, gain, w, *, tm, tn):
    n, d = x.shape
    c = w.shape[1]
    return pl.pallas_call(
        _inproj_kernel,
        grid=(n // tm, c // tn),
        in_specs=[
            pl.BlockSpec((tm, d), lambda i, j: (i, 0)),
            pl.BlockSpec((1, d), lambda i, j: (0, 0)),
            pl.BlockSpec((d, tn), lambda i, j: (0, j)),
        ],
        out_specs=pl.BlockSpec((tm, tn), lambda i, j: (i, j)),
        out_shape=jax.ShapeDtypeStruct((n, c), BF16),
        scratch_shapes=[pltpu.VMEM((tm, d), BF16)],
        compiler_params=_params("parallel", "arbitrary"),
        name="inproj",
    )(x, gain, w)


def _pool_kernel(u_ref, halo_ref, w_ref, scale_ref, o_ref, ext_scr, *, ts):
    s = pl.program_id(1)
    mix = u_ref.shape[1]
    gdim = mix // len(POOL_WINDOWS)
    cur = u_ref[...].astype(F32)
    ext_scr[0:POOL_HALO, :] = jnp.where(s > 0, halo_ref[...].astype(F32), 0.0)
    ext_scr[POOL_HALO:, :] = cur
    t = s * ts + lax.broadcasted_iota(jnp.int32, (ts, 1), 0)
    for g, win in enumerate(POOL_WINDOWS):
        cols = slice(g * gdim, (g + 1) * gdim)
        tot = ext_scr[POOL_HALO:POOL_HALO + ts, cols]
        for j in range(1, win):
            tot = tot + ext_scr[POOL_HALO - j:POOL_HALO - j + ts, cols]
        count = jnp.minimum(t + 1, win).astype(F32)
        delta = tot / count - cur[:, cols]
        y = jnp.dot(delta.astype(BF16), w_ref[g], preferred_element_type=F32)
        o_ref[:, cols] = (y * scale_ref[:, cols]).astype(o_ref.dtype)


def _pool(proj, pool_w, pool_scale, *, batch, seq, mix, ts):
    n = batch * seq
    spb = seq // ts
    hpt = ts // POOL_HALO
    g = len(POOL_WINDOWS)
    return pl.pallas_call(
        functools.partial(_pool_kernel, ts=ts),
        grid=(batch, spb),
        in_specs=[
            pl.BlockSpec((ts, mix), lambda b, s: (b * spb + s, 0)),
            pl.BlockSpec((POOL_HALO, mix),
                         lambda b, s: (jnp.maximum((b * spb + s) * hpt - 1, 0), 0)),
            pl.BlockSpec((g, mix // g, mix // g), lambda b, s: (0, 0, 0)),
            pl.BlockSpec((1, mix), lambda b, s: (0, 0)),
        ],
        out_specs=pl.BlockSpec((ts, mix), lambda b, s: (b * spb + s, 0)),
        out_shape=jax.ShapeDtypeStruct((n, mix), BF16),
        scratch_shapes=[pltpu.VMEM((ts + POOL_HALO, mix), F32)],
        compiler_params=_params("parallel", "arbitrary"),
        name="pool",
    )(proj, proj, pool_w, pool_scale)


def _bias_tiles_kernel(tbl_ref, o_ref, *, tile):
    h = pl.program_id(0)
    exact = REL_BUCKETS // 2
    row = lax.broadcasted_iota(jnp.int32, (tile, tile), 0)
    col = lax.broadcasted_iota(jnp.int32, (tile, tile), 1)
    far = tbl_ref[h * REL_BUCKETS + REL_BUCKETS - 1]
    for which in range(2):
        dist = row - col + which * tile
        n = jnp.maximum(dist, 0)
        nf = jnp.maximum(n, 1).astype(F32)
        large = exact + (jnp.log(nf / exact) / math.log(REL_MAX_DIST / exact)
                         * (REL_BUCKETS - exact)).astype(jnp.int32)
        large = jnp.minimum(large, REL_BUCKETS - 1)
        bucket = jnp.where(n < exact, n, large)
        bias = jnp.zeros((tile, tile), F32)
        for b in range(REL_BUCKETS):
            bias = jnp.where(bucket == b, tbl_ref[h * REL_BUCKETS + b], bias)
        bias = bias - far
        if which == 0:
            bias = jnp.where(dist >= 0, bias, NEG_INF)
        o_ref[0, which] = bias


def _bias_tiles(rel_bias, *, tile):
    heads = rel_bias.shape[1]
    tbl = rel_bias.astype(F32).T.reshape(-1)
    return pl.pallas_call(
        functools.partial(_bias_tiles_kernel, tile=tile),
        grid=(heads,),
        in_specs=[pl.BlockSpec(memory_space=pltpu.SMEM)],
        out_specs=pl.BlockSpec((1, 2, tile, tile), lambda h: (h, 0, 0, 0)),
        out_shape=jax.ShapeDtypeStruct((heads, 2, tile, tile), F32),
        compiler_params=_params("arbitrary"),
        name="bias_tiles",
    )(tbl)


def _diff_kernel(q_ref, k_ref, v_ref, bias_ref, lam_ref, gain_ref, o_ref,
                 kt_scr, m_scr, l_scr, acc_scr, *, tile, lambda_init):
    i = pl.program_id(2)
    seq = k_ref.shape[0]
    dq = DIFF_QK_DIM

    @pl.when(i == 0)
    def _():
        for c in range(seq // tile):
            rows = slice(c * tile, (c + 1) * tile)
            kt_scr[:, rows] = k_ref[rows, :].astype(F32).T.astype(BF16)

    m_scr[...] = jnp.full(m_scr.shape, NEG_INF, F32)
    l_scr[...] = jnp.zeros(l_scr.shape, F32)
    acc_scr[...] = jnp.zeros(acc_scr.shape, F32)

    q = q_ref[...] * (dq ** -0.5)
    qs = (q[:, :dq], q[:, dq:])

    def tile_update(start, bias):
        v = v_ref[pl.ds(start, tile), :]
        for m in range(2):
            s = jnp.dot(qs[m], kt_scr[m * dq:(m + 1) * dq, pl.ds(start, tile)],
                        preferred_element_type=F32)
            if bias is not None:
                s = s + bias
            m_prev = m_scr[m]
            m_next = jnp.maximum(m_prev, jnp.max(s, axis=1, keepdims=True))
            p = jnp.exp(s - jnp.tile(m_next, (1, tile // LANES)))
            alpha = jnp.exp(m_prev - m_next)
            l_scr[m] = alpha * l_scr[m] + jnp.sum(p, axis=1, keepdims=True)
            m_scr[m] = m_next
            acc_scr[m] = alpha * acc_scr[m] + jnp.dot(p.astype(BF16), v,
                                                      preferred_element_type=F32)

    def far_body(j, carry):
        tile_update(pl.multiple_of(j * tile, tile), None)
        return carry

    lax.fori_loop(0, jnp.maximum(i - 1, 0), far_body, 0)

    @pl.when(i > 0)
    def _():
        tile_update(pl.multiple_of((i - 1) * tile, tile), bias_ref[0, 1])

    tile_update(pl.multiple_of(i * tile, tile), bias_ref[0, 0])

    lp = lam_ref[...]
    lam = (jnp.exp(jnp.sum(lp[0:1] * lp[1:2], axis=-1, keepdims=True))
           - jnp.exp(jnp.sum(lp[2:3] * lp[3:4], axis=-1, keepdims=True)) + lambda_init)
    out = acc_scr[0] / l_scr[0] - lam * (acc_scr[1] / l_scr[1])
    out = _rmsnorm_rows(out, gain_ref[...]) * (1.0 - lambda_init)
    o_ref[...] = out.astype(o_ref.dtype)


def _diff_attention(proj, bias_tiles, lam_params, head_gain, *, batch, seq, q_col, k_col, v_col,
                    tile, lambda_init):
    n = batch * seq
    qpb = seq // tile
    heads = DIFF_HEADS
    dv = DIFF_V_DIM
    qb, kb, vb = q_col // dv, k_col // dv, v_col // dv
    return pl.pallas_call(
        functools.partial(_diff_kernel, tile=tile, lambda_init=lambda_init),
        grid=(batch, heads, qpb),
        in_specs=[
            pl.BlockSpec((tile, dv), lambda b, h, i: (b * qpb + i, qb + h)),
            pl.BlockSpec((seq, dv), lambda b, h, i: (b, kb + h)),
            pl.BlockSpec((seq, dv), lambda b, h, i: (b, vb + h)),
            pl.BlockSpec((1, 2, tile, tile), lambda b, h, i: (h, 0, 0, 0)),
            pl.BlockSpec(lam_params.shape, lambda b, h, i: (0, 0)),
            pl.BlockSpec((1, dv), lambda b, h, i: (0, 0)),
        ],
        out_specs=pl.BlockSpec((tile, dv), lambda b, h, i: (b * qpb + i, h)),
        out_shape=jax.ShapeDtypeStruct((n, heads * dv), BF16),
        scratch_shapes=[
            pltpu.VMEM((dv, seq), BF16),
            pltpu.VMEM((2, tile, LANES), F32),
            pltpu.VMEM((2, tile, LANES), F32),
            pltpu.VMEM((2, tile, dv), F32),
        ],
        compiler_params=_params("parallel", "parallel", "arbitrary"),
        name="diff_attention",
    )(proj, proj, proj, bias_tiles, lam_params, head_gain)


def _ret_log_gamma(h):
    return math.log(1.0 - 2.0 ** (-5.0 - h))


def _retention_kernel(q_ref, k_ref, v_ref, g_ref, qcos_ref, qsin_ref, kcos_ref, ksin_ref,
                      o_ref, state_scr, *, ts):
    @pl.when(pl.program_id(1) == 0)
    def _():
        state_scr[...] = jnp.zeros(state_scr.shape, F32)

    c = RET_CHUNK
    d = RET_DIM
    row = lax.broadcasted_iota(jnp.int32, (c, c), 0).astype(F32)
    col = lax.broadcasted_iota(jnp.int32, (c, c), 1).astype(F32)
    rel = row - col
    qcos, qsin = qcos_ref[...], qsin_ref[...]
    kcos, ksin = kcos_ref[...], ksin_ref[...]

    for h in range(RET_HEADS):
        lg = _ret_log_gamma(h)
        intra = jnp.where(rel >= 0, jnp.exp(jnp.maximum(rel, 0.0) * lg), 0.0)
        key_decay = jnp.exp((c - 1 - row) * lg)
        query_decay = jnp.exp((row + 1) * lg)
        chunk_decay = math.exp(c * lg)
        cols = slice(h * d, (h + 1) * d)
        qh = q_ref[:, cols].astype(F32)
        kh = k_ref[:, cols].astype(F32)
        qh = qh * qcos + pltpu.roll(qh, d // 2, axis=1) * qsin
        kh = kh * kcos + pltpu.roll(kh, d // 2, axis=1) * ksin
        for ci in range(ts // c):
            rows = slice(ci * c, (ci + 1) * c)
            qc = qh[rows].astype(BF16)
            kc = kh[rows]
            vc = v_ref[rows, cols]
            state = state_scr[h]
            scores = lax.dot_general(qc, kc.astype(BF16), (((1,), (1,)), ((), ())),
                                     preferred_element_type=F32) * intra
            inner = jnp.dot(scores.astype(BF16), vc, preferred_element_type=F32)
            cross = jnp.dot(qc, state.astype(BF16), preferred_element_type=F32) * query_decay
            kv = lax.dot_general((kc * key_decay).astype(BF16), vc, (((0,), (0,)), ((), ())),
                                 preferred_element_type=F32)
            state_scr[h] = chunk_decay * state + kv
            out = _rmsnorm_rows(inner + cross)
            gate = g_ref[rows, cols].astype(F32)
            o_ref[rows, cols] = (gate * jax.nn.sigmoid(gate) * out).astype(o_ref.dtype)


def _rotary_tables(seq):
    half = RET_DIM // 2
    inv_freq = 1.0 / (ROPE_BASE ** jnp.linspace(0.0, 1.0, half, dtype=F32))
    ang = jnp.arange(seq, dtype=F32)[:, None] * inv_freq[None, :]
    cos, sin = jnp.cos(ang), jnp.sin(ang)
    cos2 = jnp.concatenate([cos, cos], axis=-1)
    sin2 = jnp.concatenate([-sin, sin], axis=-1)
    kscale = RET_DIM ** -0.5
    return cos2, sin2, cos2 * kscale, sin2 * kscale


def _retention(proj, tables, *, batch, seq, mix, q_col, ts):
    n = batch * seq
    spb = seq // ts
    cb = q_col // mix
    act = lambda off: pl.BlockSpec((ts, mix), lambda b, s: (b * spb + s, cb + off))
    tab = pl.BlockSpec((ts, RET_DIM), lambda b, s: (s, 0))
    return pl.pallas_call(
        functools.partial(_retention_kernel, ts=ts),
        grid=(batch, spb),
        in_specs=[act(0), act(1), act(2), act(3), tab, tab, tab, tab],
        out_specs=pl.BlockSpec((ts, mix), lambda b, s: (b * spb + s, 0)),
        out_shape=jax.ShapeDtypeStruct((n, mix), BF16),
        scratch_shapes=[pltpu.VMEM((RET_HEADS, RET_DIM, RET_DIM), F32)],
        compiler_params=_params("parallel", "arbitrary"),
        name="retention",
    )(proj, proj, proj, proj, *tables)


def _merge_kernel(yp_ref, yd_ref, yr_ref, g0_ref, g1_ref, g2_ref, w_ref, o_ref):
    acc = None
    for n, (y_ref, g_ref) in enumerate(((yp_ref, g0_ref), (yd_ref, g1_ref), (yr_ref, g2_ref))):
        wide = jnp.dot(y_ref[...], w_ref[n], preferred_element_type=F32)
        term = jax.nn.sigmoid(g_ref[...].astype(F32)) * wide
        acc = term if acc is None else acc + term
    o_ref[...] = acc.astype(o_ref.dtype)


def _merge(y_pool, y_diff, y_ret, proj, w_branch, *, gate_col, tm, tn):
    n, mix = y_pool.shape
    d = w_branch.shape[2]
    y_spec = pl.BlockSpec((tm, mix), lambda i, j: (i, 0))
    gate = lambda br: pl.BlockSpec((tm, tn), lambda i, j: (i, (gate_col + br * d) // tn + j))
    return pl.pallas_call(
        _merge_kernel,
        grid=(n // tm, d // tn),
        in_specs=[y_spec, y_spec, y_spec, gate(0), gate(1), gate(2),
                  pl.BlockSpec((N_BRANCHES, mix, tn), lambda i, j: (0, 0, j))],
        out_specs=pl.BlockSpec((tm, tn), lambda i, j: (i, j)),
        out_shape=jax.ShapeDtypeStruct((n, d), BF16),
        compiler_params=_params("parallel", "arbitrary"),
        name="merge",
    )(y_pool, y_diff, y_ret, proj, proj, proj, w_branch)


def _outproj_kernel(m_ref, w_ref, x_ref, g_ref, o_ref):
    mixed = jnp.dot(m_ref[...], w_ref[...], preferred_element_type=F32)
    o_ref[...] = x_ref[...] + _rmsnorm_rows(mixed, g_ref[...])


def _outproj(merged, w_out, x, gain, *, tm):
    n, d = x.shape
    row = pl.BlockSpec((tm, d), lambda i: (i, 0))
    return pl.pallas_call(
        _outproj_kernel,
        grid=(n // tm,),
        in_specs=[row, pl.BlockSpec((d, d), lambda i: (0, 0)), row,
                  pl.BlockSpec((1, d), lambda i: (0, 0))],
        out_specs=row,
        out_shape=jax.ShapeDtypeStruct((n, d), F32),
        compiler_params=_params("parallel"),
        name="outproj",
    )(merged, w_out, x, gain)


def _mlp_kernel(x_ref, gpre_ref, wup_ref, wdown_ref, gpost_ref, o_ref, h_scr, acc_scr):
    k = pl.program_id(1)

    @pl.when(k == 0)
    def _():
        h_scr[...] = _rmsnorm_rows(x_ref[...], gpre_ref[...]).astype(BF16)
        acc_scr[...] = jnp.zeros(acc_scr.shape, F32)

    up = jnp.dot(h_scr[...], wup_ref[...], preferred_element_type=F32)
    ff = jnp.square(jnp.maximum(up, 0.0)).astype(BF16)
    acc_scr[...] += jnp.dot(ff, wdown_ref[...], preferred_element_type=F32)

    @pl.when(k == pl.num_programs(1) - 1)
    def _():
        o_ref[...] = x_ref[...] + _rmsnorm_rows(acc_scr[...], gpost_ref[...])


def _mlp(x, gain_pre, w_up, w_down, gain_post, *, tm, tf):
    n, d = x.shape
    f = w_up.shape[1]
    row = pl.BlockSpec((tm, d), lambda i, k: (i, 0))
    vec = pl.BlockSpec((1, d), lambda i, k: (0, 0))
    return pl.pallas_call(
        _mlp_kernel,
        grid=(n // tm, f // tf),
        in_specs=[row, vec, pl.BlockSpec((d, tf), lambda i, k: (0, k)),
                  pl.BlockSpec((tf, d), lambda i, k: (k, 0)), vec],
        out_specs=row,
        out_shape=jax.ShapeDtypeStruct((n, d), F32),
        scratch_shapes=[pltpu.VMEM((tm, d), BF16), pltpu.VMEM((tm, d), F32)],
        compiler_params=_params("parallel", "arbitrary"),
        name="mlp",
    )(x, gain_pre, w_up, w_down, gain_post)


def _tiles(n_tokens, seq, in_cols, d_model, d_ff):
    pick = lambda total, want: want if total % want == 0 else total
    return dict(
        inproj=dict(tm=pick(n_tokens, 1024), tn=pick(in_cols, 1024)),
        pool=dict(ts=pick(seq, 512)),
        diff=dict(tile=pick(seq, 256)),
        ret=dict(ts=pick(seq, 512)),
        merge=dict(tm=pick(n_tokens, 512), tn=pick(d_model, 1024)),
        outproj=dict(tm=pick(n_tokens, 512)),
        mlp=dict(tm=pick(n_tokens, 512), tf=pick(d_ff, 1024)),
    )


def kernel(x, rel_bias, norm_pre_mix, norm_post_mix, norm_pre_mlp, norm_post_mlp, w_in, pool_w,
           pool_scale, diff_lambda, diff_head_norm, w_branch, w_out, w_up, w_down):
    batch, seq, d_model = x.shape
    depth, _, in_cols = w_in.shape
    mix = pool_scale.shape[1]
    d_ff = w_up.shape[2]
    n = batch * seq
    assert mix == DIFF_HEADS * DIFF_V_DIM == RET_HEADS * RET_DIM
    assert in_cols == 8 * mix + N_BRANCHES * d_model
    tiles = _tiles(n, seq, in_cols, d_model, d_ff)

    col = lambda idx: idx * mix
    bias_tiles = _bias_tiles(rel_bias, tile=tiles["diff"]["tile"])
    tables = _rotary_tables(seq)
    vec = lambda a: a.astype(F32).reshape(1, -1)

    xs = x.astype(F32).reshape(n, d_model)
    for l in range(depth):
        lambda_init = 0.8 - 0.6 * math.exp(-0.3 * l)
        proj = _inproj(xs, vec(norm_pre_mix[l]), w_in[l].astype(BF16), **tiles["inproj"])
        y_pool = _pool(proj, pool_w[l].astype(BF16), vec(pool_scale[l]),
                       batch=batch, seq=seq, mix=mix, **tiles["pool"])
        y_diff = _diff_attention(proj, bias_tiles, diff_lambda[l].astype(F32),
                                 vec(diff_head_norm[l]), batch=batch, seq=seq,
                                 q_col=col(1), k_col=col(2), v_col=col(3),
                                 lambda_init=lambda_init, **tiles["diff"])
        y_ret = _retention(proj, tables, batch=batch, seq=seq, mix=mix, q_col=col(4),
                           **tiles["ret"])
        merged = _merge(y_pool, y_diff, y_ret, proj, w_branch[l].astype(BF16),
                        gate_col=col(8), **tiles["merge"])
        xs = _outproj(merged, w_out[l].astype(BF16), xs, vec(norm_post_mix[l]),
                      **tiles["outproj"])
        xs = _mlp(xs, vec(norm_pre_mlp[l]), w_up[l].astype(BF16), w_down[l].astype(BF16),
                  vec(norm_post_mlp[l]), **tiles["mlp"])
    return xs.reshape(batch, seq, d_model).astype(x.dtype)
```

```python
import functools
import math

import jax
import jax.numpy as jnp
from jax import lax
from jax.experimental import pallas as pl
from jax.experimental.pallas import tpu as pltpu

F32 = jnp.float32
BF16 = jnp.bfloat16

LANES = 128
POOL_WINDOWS = (2, 4, 8, 16)
POOL_HALO = 16
DIFF_HEADS = 8
DIFF_QK_DIM = 64
DIFF_V_DIM = 128
RET_HEADS = 8
RET_DIM = 128
RET_CHUNK = 128
ROPE_BASE = 10000.0
N_BRANCHES = 3
REL_BUCKETS = 32
REL_MAX_DIST = 128
NORM_EPS = 1e-6
NEG_INF = -1e30
VMEM_LIMIT = 56 * 1024 * 1024


def _params(*semantics):
    return pltpu.CompilerParams(dimension_semantics=semantics, vmem_limit_bytes=VMEM_LIMIT)


def _rmsnorm_rows(x, gain=None):
    y = x * lax.rsqrt(jnp.mean(x * x, axis=-1, keepdims=True) + NORM_EPS)
    return y if gain is None else y * gain


def _inproj_kernel(x_ref, g_ref, w_ref, o_ref, a_scr):
    @pl.when(pl.program_id(1) == 0)
    def _():
        a_scr[...] = _rmsnorm_rows(x_ref[...], g_ref[...]).astype(BF16)

    o_ref[...] = jnp.dot(a_scr[...], w_ref[...], preferred_element_type=F32).astype(o_ref.dtype)


def _inproj(x, gain, w, *, tm, tn):
    n, d = x.shape
    c = w.shape[1]
    return pl.pallas_call(
        _inproj_kernel,
        grid=(n // tm, c // tn),
        in_specs=[
            pl.BlockSpec((tm, d), lambda i, j: (i, 0)),
            pl.BlockSpec((1, d), lambda i, j: (0, 0)),
            pl.BlockSpec((d, tn), lambda i, j: (0, j)),
        ],
        out_specs=pl.BlockSpec((tm, tn), lambda i, j: (i, j)),
        out_shape=jax.ShapeDtypeStruct((n, c), BF16),
        scratch_shapes=[pltpu.VMEM((tm, d), BF16)],
        compiler_params=_params("parallel", "arbitrary"),
        name="inproj",
    )(x, gain, w)


def _pool_kernel(u_ref, halo_ref, w_ref, scale_ref, o_ref, ext_scr, *, ts):
    s = pl.program_id(1)
    mix = u_ref.shape[1]
    gdim = mix // len(POOL_WINDOWS)
    cur = u_ref[...].astype(F32)
    ext_scr[0:POOL_HALO, :] = jnp.where(s > 0, halo_ref[...].astype(F32), 0.0)
    ext_scr[POOL_HALO:, :] = cur
    t = s * ts + lax.broadcasted_iota(jnp.int32, (ts, 1), 0)
    for g, win in enumerate(POOL_WINDOWS):
        cols = slice(g * gdim, (g + 1) * gdim)
        tot = ext_scr[POOL_HALO:POOL_HALO + ts, cols]
        for j in range(1, win):
            tot = tot + ext_scr[POOL_HALO - j:POOL_HALO - j + ts, cols]
        count = jnp.minimum(t + 1, win).astype(F32)
        delta = tot / count - cur[:, cols]
        y = jnp.dot(delta.astype(BF16), w_ref[g], preferred_element_type=F32)
        o_ref[:, cols] = (y * scale_ref[:, cols]).astype(o_ref.dtype)


def _pool(proj, pool_w, pool_scale, *, batch, seq, mix, ts):
    n = batch * seq
    spb = seq // ts
    hpt = ts // POOL_HALO
    g = len(POOL_WINDOWS)
    return pl.pallas_call(
        functools.partial(_pool_kernel, ts=ts),
        grid=(batch, spb),
        in_specs=[
            pl.BlockSpec((ts, mix), lambda b, s: (b * spb + s, 0)),
            pl.BlockSpec((POOL_HALO, mix),
                         lambda b, s: (jnp.maximum((b * spb + s) * hpt - 1, 0), 0)),
            pl.BlockSpec((g, mix // g, mix // g), lambda b, s: (0, 0, 0)),
            pl.BlockSpec((1, mix), lambda b, s: (0, 0)),
        ],
        out_specs=pl.BlockSpec((ts, mix), lambda b, s: (b * spb + s, 0)),
        out_shape=jax.ShapeDtypeStruct((n, mix), BF16),
        scratch_shapes=[pltpu.VMEM((ts + POOL_HALO, mix), F32)],
        compiler_params=_params("parallel", "arbitrary"),
        name="pool",
    )(proj, proj, pool_w, pool_scale)


def _bias_tiles_kernel(tbl_ref, o_ref, *, tile):
    h = pl.program_id(0)
    exact = REL_BUCKETS // 2
    row = lax.broadcasted_iota(jnp.int32, (tile, tile), 0)
    col = lax.broadcasted_iota(jnp.int32, (tile, tile), 1)
    far = tbl_ref[h * REL_BUCKETS + REL_BUCKETS - 1]
    for which in range(2):
        dist = row - col + which * tile
        n = jnp.maximum(dist, 0)
        nf = jnp.maximum(n, 1).astype(F32)
        large = exact + (jnp.log(nf / exact) / math.log(REL_MAX_DIST / exact)
                         * (REL_BUCKETS - exact)).astype(jnp.int32)
        large = jnp.minimum(large, REL_BUCKETS - 1)
        bucket = jnp.where(n < exact, n, large)
        bias = jnp.zeros((tile, tile), F32)
        for b in range(REL_BUCKETS):
            bias = jnp.where(bucket == b, tbl_ref[h * REL_BUCKETS + b], bias)
        bias = bias - far
        if which == 0:
            bias = jnp.where(dist >= 0, bias, NEG_INF)
        o_ref[0, :, (1 - which) * tile:(2 - which) * tile] = bias


def _bias_tiles(rel_bias, *, tile):
    heads = rel_bias.shape[1]
    tbl = rel_bias.astype(F32).T.reshape(-1)
    return pl.pallas_call(
        functools.partial(_bias_tiles_kernel, tile=tile),
        grid=(heads,),
        in_specs=[pl.BlockSpec(memory_space=pltpu.SMEM)],
        out_specs=pl.BlockSpec((1, tile, 2 * tile), lambda h: (h, 0, 0)),
        out_shape=jax.ShapeDtypeStruct((heads, tile, 2 * tile), F32),
        compiler_params=_params("arbitrary"),
        name="bias_tiles",
    )(tbl)


def _diff_kernel(q_ref, k_ref, v_ref, bias_ref, lam_ref, gain_ref, o_ref,
                 kt_scr, m_scr, l_scr, acc_scr, *, tile, lambda_init):
    i = pl.program_id(2)
    seq = k_ref.shape[0]
    dq = DIFF_QK_DIM

    @pl.when(i == 0)
    def _():
        for c in range(seq // tile):
            rows = slice(c * tile, (c + 1) * tile)
            kt_scr[:, rows] = k_ref[rows, :].astype(F32).T.astype(BF16)

    m_scr[...] = jnp.full(m_scr.shape, NEG_INF, F32)
    l_scr[...] = jnp.zeros(l_scr.shape, F32)
    acc_scr[...] = jnp.zeros(acc_scr.shape, F32)

    q = q_ref[...] * (dq ** -0.5)
    qs = (q[:, :dq], q[:, dq:])

    def tile_update(start, width, bias):
        v = v_ref[pl.ds(start, width), :]
        for m in range(2):
            s = jnp.dot(qs[m], kt_scr[m * dq:(m + 1) * dq, pl.ds(start, width)],
                        preferred_element_type=F32)
            if bias is not None:
                s = s + bias
            m_prev = m_scr[m]
            m_next = jnp.maximum(m_prev, jnp.max(s, axis=1, keepdims=True))
            p = jnp.exp(s - jnp.tile(m_next, (1, width // LANES)))
            alpha = jnp.exp(m_prev - m_next)
            l_scr[m] = alpha * l_scr[m] + jnp.sum(p, axis=1, keepdims=True)
            m_scr[m] = m_next
            acc_scr[m] = alpha * acc_scr[m] + jnp.dot(p.astype(BF16), v,
                                                      preferred_element_type=F32)

    n_far = jnp.maximum(i - 1, 0)

    def far_body(j, carry):
        tile_update(pl.multiple_of(j * (2 * tile), 2 * tile), 2 * tile, None)
        return carry

    lax.fori_loop(0, n_far // 2, far_body, 0)

    @pl.when(n_far % 2 == 1)
    def _():
        tile_update(pl.multiple_of((n_far - 1) * tile, tile), tile, None)

    @pl.when(i > 0)
    def _():
        tile_update(pl.multiple_of((i - 1) * tile, tile), 2 * tile, bias_ref[0])

    @pl.when(i == 0)
    def _():
        tile_update(0, tile, bias_ref[0, :, tile:])

    lp = lam_ref[...]
    lam = (jnp.exp(jnp.sum(lp[0:1] * lp[1:2], axis=-1, keepdims=True))
           - jnp.exp(jnp.sum(lp[2:3] * lp[3:4], axis=-1, keepdims=True)) + lambda_init)
    out = acc_scr[0] / l_scr[0] - lam * (acc_scr[1] / l_scr[1])
    out = _rmsnorm_rows(out, gain_ref[...]) * (1.0 - lambda_init)
    o_ref[...] = out.astype(o_ref.dtype)


def _diff_attention(proj, bias_tiles, lam_params, head_gain, *, batch, seq, q_col, k_col, v_col,
                    tile, lambda_init):
    n = batch * seq
    qpb = seq // tile
    heads = DIFF_HEADS
    dv = DIFF_V_DIM
    qb, kb, vb = q_col // dv, k_col // dv, v_col // dv
    return pl.pallas_call(
        functools.partial(_diff_kernel, tile=tile, lambda_init=lambda_init),
        grid=(batch, heads, qpb),
        in_specs=[
            pl.BlockSpec((tile, dv), lambda b, h, i: (b * qpb + i, qb + h)),
            pl.BlockSpec((seq, dv), lambda b, h, i: (b, kb + h)),
            pl.BlockSpec((seq, dv), lambda b, h, i: (b, vb + h)),
            pl.BlockSpec((1, tile, 2 * tile), lambda b, h, i: (h, 0, 0)),
            pl.BlockSpec(lam_params.shape, lambda b, h, i: (0, 0)),
            pl.BlockSpec((1, dv), lambda b, h, i: (0, 0)),
        ],
        out_specs=pl.BlockSpec((tile, dv), lambda b, h, i: (b * qpb + i, h)),
        out_shape=jax.ShapeDtypeStruct((n, heads * dv), BF16),
        scratch_shapes=[
            pltpu.VMEM((dv, seq), BF16),
            pltpu.VMEM((2, tile, LANES), F32),
            pltpu.VMEM((2, tile, LANES), F32),
            pltpu.VMEM((2, tile, dv), F32),
        ],
        compiler_params=_params("parallel", "parallel", "arbitrary"),
        name="diff_attention",
    )(proj, proj, proj, bias_tiles, lam_params, head_gain)


def _ret_log_gamma(h):
    return math.log(1.0 - 2.0 ** (-5.0 - h))


def _retention_kernel(q_ref, k_ref, v_ref, g_ref, qcos_ref, qsin_ref, kcos_ref, ksin_ref,
                      o_ref, state_scr, *, ts):
    @pl.when(pl.program_id(1) == 0)
    def _():
        state_scr[...] = jnp.zeros(state_scr.shape, F32)

    c = RET_CHUNK
    d = RET_DIM
    row = lax.broadcasted_iota(jnp.int32, (c, c), 0).astype(F32)
    col = lax.broadcasted_iota(jnp.int32, (c, c), 1).astype(F32)
    rel = row - col
    qcos, qsin = qcos_ref[...], qsin_ref[...]
    kcos, ksin = kcos_ref[...], ksin_ref[...]

    for h in range(RET_HEADS):
        lg = _ret_log_gamma(h)
        intra = jnp.where(rel >= 0, jnp.exp(jnp.maximum(rel, 0.0) * lg), 0.0)
        key_decay = jnp.exp((c - 1 - row) * lg)
        query_decay = jnp.exp((row + 1) * lg)
        chunk_decay = math.exp(c * lg)
        cols = slice(h * d, (h + 1) * d)
        qh = q_ref[:, cols].astype(F32)
        kh = k_ref[:, cols].astype(F32)
        qh = qh * qcos + pltpu.roll(qh, d // 2, axis=1) * qsin
        kh = kh * kcos + pltpu.roll(kh, d // 2, axis=1) * ksin
        for ci in range(ts // c):
            rows = slice(ci * c, (ci + 1) * c)
            qc = qh[rows].astype(BF16)
            kc = kh[rows]
            vc = v_ref[rows, cols]
            state = state_scr[h]
            scores = lax.dot_general(qc, kc.astype(BF16), (((1,), (1,)), ((), ())),
                                     preferred_element_type=F32) * intra
            inner = jnp.dot(scores.astype(BF16), vc, preferred_element_type=F32)
            cross = jnp.dot(qc, state.astype(BF16), preferred_element_type=F32) * query_decay
            kv = lax.dot_general((kc * key_decay).astype(BF16), vc, (((0,), (0,)), ((), ())),
                                 preferred_element_type=F32)
            state_scr[h] = chunk_decay * state + kv
            out = _rmsnorm_rows(inner + cross)
            gate = g_ref[rows, cols].astype(F32)
            o_ref[rows, cols] = (gate * jax.nn.sigmoid(gate) * out).astype(o_ref.dtype)


def _rotary_tables(seq):
    half = RET_DIM // 2
    inv_freq = 1.0 / (ROPE_BASE ** jnp.linspace(0.0, 1.0, half, dtype=F32))
    ang = jnp.arange(seq, dtype=F32)[:, None] * inv_freq[None, :]
    cos, sin = jnp.cos(ang), jnp.sin(ang)
    cos2 = jnp.concatenate([cos, cos], axis=-1)
    sin2 = jnp.concatenate([-sin, sin], axis=-1)
    kscale = RET_DIM ** -0.5
    return cos2, sin2, cos2 * kscale, sin2 * kscale


def _retention(proj, tables, *, batch, seq, mix, q_col, ts):
    n = batch * seq
    spb = seq // ts
    cb = q_col // mix
    act = lambda off: pl.BlockSpec((ts, mix), lambda b, s: (b * spb + s, cb + off))
    tab = pl.BlockSpec((ts, RET_DIM), lambda b, s: (s, 0))
    return pl.pallas_call(
        functools.partial(_retention_kernel, ts=ts),
        grid=(batch, spb),
        in_specs=[act(0), act(1), act(2), act(3), tab, tab, tab, tab],
        out_specs=pl.BlockSpec((ts, mix), lambda b, s: (b * spb + s, 0)),
        out_shape=jax.ShapeDtypeStruct((n, mix), BF16),
        scratch_shapes=[pltpu.VMEM((RET_HEADS, RET_DIM, RET_DIM), F32)],
        compiler_params=_params("parallel", "arbitrary"),
        name="retention",
    )(proj, proj, proj, proj, *tables)


def _merge_kernel(yp_ref, yd_ref, yr_ref, g0_ref, g1_ref, g2_ref, w_ref, o_ref):
    acc = None
    for n, (y_ref, g_ref) in enumerate(((yp_ref, g0_ref), (yd_ref, g1_ref), (yr_ref, g2_ref))):
        wide = jnp.dot(y_ref[...], w_ref[n], preferred_element_type=F32)
        term = jax.nn.sigmoid(g_ref[...].astype(F32)) * wide
        acc = term if acc is None else acc + term
    o_ref[...] = acc.astype(o_ref.dtype)


def _merge(y_pool, y_diff, y_ret, proj, w_branch, *, gate_col, tm, tn):
    n, mix = y_pool.shape
    d = w_branch.shape[2]
    y_spec = pl.BlockSpec((tm, mix), lambda i, j: (i, 0))
    gate = lambda br: pl.BlockSpec((tm, tn), lambda i, j: (i, (gate_col + br * d) // tn + j))
    return pl.pallas_call(
        _merge_kernel,
        grid=(n // tm, d // tn),
        in_specs=[y_spec, y_spec, y_spec, gate(0), gate(1), gate(2),
                  pl.BlockSpec((N_BRANCHES, mix, tn), lambda i, j: (0, 0, j))],
        out_specs=pl.BlockSpec((tm, tn), lambda i, j: (i, j)),
        out_shape=jax.ShapeDtypeStruct((n, d), BF16),
        compiler_params=_params("parallel", "arbitrary"),
        name="merge",
    )(y_pool, y_diff, y_ret, proj, proj, proj, w_branch)


def _outproj_kernel(m_ref, w_ref, x_ref, g_ref, o_ref):
    mixed = jnp.dot(m_ref[...], w_ref[...], preferred_element_type=F32)
    o_ref[...] = x_ref[...] + _rmsnorm_rows(mixed, g_ref[...])


def _outproj(merged, w_out, x, gain, *, tm):
    n, d = x.shape
    row = pl.BlockSpec((tm, d), lambda i: (i, 0))
    return pl.pallas_call(
        _outproj_kernel,
        grid=(n // tm,),
        in_specs=[row, pl.BlockSpec((d, d), lambda i: (0, 0)), row,
                  pl.BlockSpec((1, d), lambda i: (0, 0))],
        out_specs=row,
        out_shape=jax.ShapeDtypeStruct((n, d), F32),
        compiler_params=_params("parallel"),
        name="outproj",
    )(merged, w_out, x, gain)


def _mlp_kernel(x_ref, gpre_ref, wup_ref, wdown_ref, gpost_ref, o_ref, h_scr, acc_scr):
    k = pl.program_id(1)

    @pl.when(k == 0)
    def _():
        h_scr[...] = _rmsnorm_rows(x_ref[...], gpre_ref[...]).astype(BF16)
        acc_scr[...] = jnp.zeros(acc_scr.shape, F32)

    up = jnp.dot(h_scr[...], wup_ref[...], preferred_element_type=F32)
    ff = jnp.square(jnp.maximum(up, 0.0)).astype(BF16)
    acc_scr[...] += jnp.dot(ff, wdown_ref[...], preferred_element_type=F32)

    @pl.when(k == pl.num_programs(1) - 1)
    def _():
        o_ref[...] = x_ref[...] + _rmsnorm_rows(acc_scr[...], gpost_ref[...])


def _mlp(x, gain_pre, w_up, w_down, gain_post, *, tm, tf):
    n, d = x.shape
    f = w_up.shape[1]
    row = pl.BlockSpec((tm, d), lambda i, k: (i, 0))
    vec = pl.BlockSpec((1, d), lambda i, k: (0, 0))
    return pl.pallas_call(
        _mlp_kernel,
        grid=(n // tm, f // tf),
        in_specs=[row, vec, pl.BlockSpec((d, tf), lambda i, k: (0, k)),
                  pl.BlockSpec((tf, d), lambda i, k: (k, 0)), vec],
        out_specs=row,
        out_shape=jax.ShapeDtypeStruct((n, d), F32),
        scratch_shapes=[pltpu.VMEM((tm, d), BF16), pltpu.VMEM((tm, d), F32)],
        compiler_params=_params("parallel", "arbitrary"),
        name="mlp",
    )(x, gain_pre, w_up, w_down, gain_post)


def _tiles(n_tokens, seq, in_cols, d_model, d_ff):
    pick = lambda total, want: want if total % want == 0 else total
    return dict(
        inproj=dict(tm=pick(n_tokens, 1024), tn=pick(in_cols, 1024)),
        pool=dict(ts=pick(seq, 512)),
        diff=dict(tile=pick(seq, 512)),
        ret=dict(ts=pick(seq, 512)),
        merge=dict(tm=pick(n_tokens, 512), tn=pick(d_model, 1024)),
        outproj=dict(tm=pick(n_tokens, 512)),
        mlp=dict(tm=pick(n_tokens, 512), tf=pick(d_ff, 1024)),
    )


def kernel(x, rel_bias, norm_pre_mix, norm_post_mix, norm_pre_mlp, norm_post_mlp, w_in, pool_w,
           pool_scale, diff_lambda, diff_head_norm, w_branch, w_out, w_up, w_down):
    batch, seq, d_model = x.shape
    depth, _, in_cols = w_in.shape
    mix = pool_scale.shape[1]
    d_ff = w_up.shape[2]
    n = batch * seq
    assert mix == DIFF_HEADS * DIFF_V_DIM == RET_HEADS * RET_DIM
    assert in_cols == 8 * mix + N_BRANCHES * d_model
    tiles = _tiles(n, seq, in_cols, d_model, d_ff)

    col = lambda idx: idx * mix
    bias_tiles = _bias_tiles(rel_bias, tile=tiles["diff"]["tile"])
    tables = _rotary_tables(seq)
    vec = lambda a: a.astype(F32).reshape(1, -1)

    xs = x.astype(F32).reshape(n, d_model)
    for l in range(depth):
        lambda_init = 0.8 - 0.6 * math.exp(-0.3 * l)
        proj = _inproj(xs, vec(norm_pre_mix[l]), w_in[l].astype(BF16), **tiles["inproj"])
        y_pool = _pool(proj, pool_w[l].astype(BF16), vec(pool_scale[l]),
                       batch=batch, seq=seq, mix=mix, **tiles["pool"])
        y_diff = _diff_attention(proj, bias_tiles, diff_lambda[l].astype(F32),
                                 vec(diff_head_norm[l]), batch=batch, seq=seq,
                                 q_col=col(1), k_col=col(2), v_col=col(3),
                                 lambda_init=lambda_init, **tiles["diff"])
        y_ret = _retention(proj, tables, batch=batch, seq=seq, mix=mix, q_col=col(4),
                           **tiles["ret"])
        merged = _merge(y_pool, y_diff, y_ret, proj, w_branch[l].astype(BF16),
                        gate_col=col(8), **tiles["merge"])
        xs = _outproj(merged, w_out[l].astype(BF16), xs, vec(norm_post_mix[l]),
                      **tiles["outproj"])
        xs = _mlp(xs, vec(norm_pre_mlp[l]), w_up[l].astype(BF16), w_down[l].astype(BF16),
                  vec(norm_post_mlp[l]), **tiles["mlp"])
    return xs.reshape(batch, seq, d_model).astype(x.dtype)
```

```python
import functools
import math

import jax
import jax.numpy as jnp
from jax import lax
from jax.experimental import pallas as pl
from jax.experimental.pallas import tpu as pltpu

F32 = jnp.float32
BF16 = jnp.bfloat16

LANES = 128
POOL_WINDOWS = (2, 4, 8, 16)
POOL_HALO = 16
DIFF_HEADS = 8
DIFF_QK_DIM = 64
DIFF_V_DIM = 128
DIFF_HEADS_PER_STEP = 2
RET_HEADS = 8
RET_DIM = 128
RET_CHUNK = 128
ROPE_BASE = 10000.0
N_BRANCHES = 3
REL_BUCKETS = 32
REL_MAX_DIST = 128
NORM_EPS = 1e-6
NEG_INF = -1e30
VMEM_LIMIT = 56 * 1024 * 1024


def _params(*semantics):
    return pltpu.CompilerParams(dimension_semantics=semantics, vmem_limit_bytes=VMEM_LIMIT)


def _rmsnorm_rows(x, gain=None):
    y = x * lax.rsqrt(jnp.mean(x * x, axis=-1, keepdims=True) + NORM_EPS)
    return y if gain is None else y * gain


def _inproj_kernel(x_ref, g_ref, w_ref, o_ref, a_scr):
    @pl.when(pl.program_id(1) == 0)
    def _():
        a_scr[...] = _rmsnorm_rows(x_ref[...], g_ref[...]).astype(BF16)

    o_ref[...] = jnp.dot(a_scr[...], w_ref[...], preferred_element_type=F32).astype(o_ref.dtype)


def _inproj(x, gain, w, layer, *, tm, tn):
    n, d = x.shape
    c = w.shape[2]
    return pl.pallas_call(
        _inproj_kernel,
        grid=(n // tm, c // tn),
        in_specs=[
            pl.BlockSpec((tm, d), lambda i, j: (i, 0)),
            pl.BlockSpec((1, d), lambda i, j: (0, 0)),
            pl.BlockSpec((None, d, tn), lambda i, j: (layer, 0, j)),
        ],
        out_specs=pl.BlockSpec((tm, tn), lambda i, j: (i, j)),
        out_shape=jax.ShapeDtypeStruct((n, c), BF16),
        scratch_shapes=[pltpu.VMEM((tm, d), BF16)],
        compiler_params=_params("parallel", "arbitrary"),
        name="inproj",
    )(x, gain, w)


def _pool_kernel(u_ref, halo_ref, w_ref, scale_ref, o_ref, ext_scr, *, ts):
    s = pl.program_id(1)
    mix = u_ref.shape[1]
    gdim = mix // len(POOL_WINDOWS)
    cur = u_ref[...].astype(F32)
    ext_scr[0:POOL_HALO, :] = jnp.where(s > 0, halo_ref[...].astype(F32), 0.0)
    ext_scr[POOL_HALO:, :] = cur
    t = s * ts + lax.broadcasted_iota(jnp.int32, (ts, 1), 0)
    for g, win in enumerate(POOL_WINDOWS):
        cols = slice(g * gdim, (g + 1) * gdim)
        tot = ext_scr[POOL_HALO:POOL_HALO + ts, cols]
        for j in range(1, win):
            tot = tot + ext_scr[POOL_HALO - j:POOL_HALO - j + ts, cols]
        count = jnp.minimum(t + 1, win).astype(F32)
        delta = tot / count - cur[:, cols]
        y = jnp.dot(delta.astype(BF16), w_ref[g], preferred_element_type=F32)
        o_ref[:, cols] = (y * scale_ref[:, cols]).astype(o_ref.dtype)


def _pool(proj, pool_w, pool_scale, layer, *, batch, seq, mix, ts):
    n = batch * seq
    spb = seq // ts
    hpt = ts // POOL_HALO
    g = len(POOL_WINDOWS)
    return pl.pallas_call(
        functools.partial(_pool_kernel, ts=ts),
        grid=(batch, spb),
        in_specs=[
            pl.BlockSpec((ts, mix), lambda b, s: (b * spb + s, 0)),
            pl.BlockSpec((POOL_HALO, mix),
                         lambda b, s: (jnp.maximum((b * spb + s) * hpt - 1, 0), 0)),
            pl.BlockSpec((None, g, mix // g, mix // g), lambda b, s: (layer, 0, 0, 0)),
            pl.BlockSpec((1, mix), lambda b, s: (0, 0)),
        ],
        out_specs=pl.BlockSpec((ts, mix), lambda b, s: (b * spb + s, 0)),
        out_shape=jax.ShapeDtypeStruct((n, mix), BF16),
        scratch_shapes=[pltpu.VMEM((ts + POOL_HALO, mix), F32)],
        compiler_params=_params("parallel", "arbitrary"),
        name="pool",
    )(proj, proj, pool_w, pool_scale)


def _bias_tiles_kernel(tbl_ref, o_ref, *, tile):
    h = pl.program_id(0)
    exact = REL_BUCKETS // 2
    row = lax.broadcasted_iota(jnp.int32, (tile, tile), 0)
    col = lax.broadcasted_iota(jnp.int32, (tile, tile), 1)
    far = tbl_ref[h * REL_BUCKETS + REL_BUCKETS - 1]
    for which in range(2):
        dist = row - col + which * tile
        n = jnp.maximum(dist, 0)
        nf = jnp.maximum(n, 1).astype(F32)
        large = exact + (jnp.log(nf / exact) / math.log(REL_MAX_DIST / exact)
                         * (REL_BUCKETS - exact)).astype(jnp.int32)
        large = jnp.minimum(large, REL_BUCKETS - 1)
        bucket = jnp.where(n < exact, n, large)
        bias = jnp.zeros((tile, tile), F32)
        for b in range(REL_BUCKETS):
            bias = jnp.where(bucket == b, tbl_ref[h * REL_BUCKETS + b], bias)
        bias = bias - far
        if which == 0:
            bias = jnp.where(dist >= 0, bias, NEG_INF)
        o_ref[0, :, (1 - which) * tile:(2 - which) * tile] = bias


def _bias_tiles(rel_bias, *, tile):
    heads = rel_bias.shape[1]
    tbl = rel_bias.astype(F32).T.reshape(-1)
    return pl.pallas_call(
        functools.partial(_bias_tiles_kernel, tile=tile),
        grid=(heads,),
        in_specs=[pl.BlockSpec(memory_space=pltpu.SMEM)],
        out_specs=pl.BlockSpec((1, tile, 2 * tile), lambda h: (h, 0, 0)),
        out_shape=jax.ShapeDtypeStruct((heads, tile, 2 * tile), F32),
        compiler_params=_params("arbitrary"),
        name="bias_tiles",
    )(tbl)


def _diff_kernel(q_ref, k_ref, v_ref, bias_ref, lam_ref, gain_ref, o_ref,
                 kt_scr, v1_scr, m_scr, acc_scr, *, tile, lambda_init):
    i = pl.program_id(2)
    seq = k_ref.shape[0]
    dq = DIFF_QK_DIM
    dv = DIFF_V_DIM
    heads = range(DIFF_HEADS_PER_STEP)

    @pl.when(i == 0)
    def _():
        for hh in heads:
            for c in range(seq // tile):
                rows = slice(c * tile, (c + 1) * tile)
                kt_scr[hh, :, rows] = (
                    k_ref[rows, hh * dv:(hh + 1) * dv].astype(F32).T.astype(BF16))
            v1_scr[hh, :, :dv] = v_ref[:, hh * dv:(hh + 1) * dv]
            v1_scr[hh, :, dv:] = jnp.ones((seq, LANES), BF16)

    m_scr[...] = jnp.full(m_scr.shape, NEG_INF, F32)
    acc_scr[...] = jnp.zeros(acc_scr.shape, F32)

    q = q_ref[...] * (dq ** -0.5)
    qs = [q[:, c * dq:(c + 1) * dq] for c in range(2 * DIFF_HEADS_PER_STEP)]

    def tile_update(start, width, biased):
        scores = [jnp.dot(qs[2 * hh + m], kt_scr[hh, m * dq:(m + 1) * dq, pl.ds(start, width)],
                          preferred_element_type=F32) for hh in heads for m in range(2)]
        for hh in heads:
            v1 = v1_scr[hh, pl.ds(start, width), :]
            for m in range(2):
                c = 2 * hh + m
                s = scores[c]
                if biased:
                    s = bias_ref[hh, :, 2 * tile - width:] + s
                m_prev = m_scr[c]
                m_next = jnp.maximum(m_prev, jnp.max(s, axis=1, keepdims=True))
                p = jnp.exp(s - jnp.tile(m_next, (1, width // LANES))).astype(BF16)
                alpha = jnp.exp(m_prev - m_next)
                m_scr[c] = m_next
                acc_scr[c] = jnp.tile(alpha, (1, 2)) * acc_scr[c] + jnp.dot(
                    p, v1, preferred_element_type=F32)

    n_far = jnp.maximum(i - 1, 0)

    def far_body(j, carry):
        tile_update(pl.multiple_of(j * (2 * tile), 2 * tile), 2 * tile, False)
        return carry

    lax.fori_loop(0, n_far // 2, far_body, 0)

    @pl.when(n_far % 2 == 1)
    def _():
        tile_update(pl.multiple_of((n_far - 1) * tile, tile), tile, False)

    @pl.when(i > 0)
    def _():
        tile_update(pl.multiple_of((i - 1) * tile, tile), 2 * tile, True)

    @pl.when(i == 0)
    def _():
        tile_update(0, tile, True)

    lp = lam_ref[...]
    lam = (jnp.exp(jnp.sum(lp[0:1] * lp[1:2], axis=-1, keepdims=True))
           - jnp.exp(jnp.sum(lp[2:3] * lp[3:4], axis=-1, keepdims=True)) + lambda_init)
    for hh in heads:
        c0, c1 = 2 * hh, 2 * hh + 1
        out = (acc_scr[c0, :, :dv] / acc_scr[c0, :, dv:]
               - lam * (acc_scr[c1, :, :dv] / acc_scr[c1, :, dv:]))
        out = _rmsnorm_rows(out, gain_ref[...]) * (1.0 - lambda_init)
        o_ref[:, hh * dv:(hh + 1) * dv] = out.astype(o_ref.dtype)


def _diff_attention(proj, bias_tiles, lam_params, head_gain, *, batch, seq, q_col, k_col, v_col,
                    tile, lambda_init):
    n = batch * seq
    qpb = seq // tile
    hps = DIFF_HEADS_PER_STEP
    dv = DIFF_V_DIM
    gw = hps * dv
    qb, kb, vb = q_col // gw, k_col // gw, v_col // gw
    maps = 2 * hps
    return pl.pallas_call(
        functools.partial(_diff_kernel, tile=tile, lambda_init=lambda_init),
        grid=(batch, DIFF_HEADS // hps, qpb),
        in_specs=[
            pl.BlockSpec((tile, gw), lambda b, g, i: (b * qpb + i, qb + g)),
            pl.BlockSpec((seq, gw), lambda b, g, i: (b, kb + g)),
            pl.BlockSpec((seq, gw), lambda b, g, i: (b, vb + g)),
            pl.BlockSpec((hps, tile, 2 * tile), lambda b, g, i: (g, 0, 0)),
            pl.BlockSpec(lam_params.shape, lambda b, g, i: (0, 0)),
            pl.BlockSpec((1, dv), lambda b, g, i: (0, 0)),
        ],
        out_specs=pl.BlockSpec((tile, gw), lambda b, g, i: (b * qpb + i, g)),
        out_shape=jax.ShapeDtypeStruct((n, DIFF_HEADS * dv), BF16),
        scratch_shapes=[
            pltpu.VMEM((hps, dv, seq), BF16),
            pltpu.VMEM((hps, seq, dv + LANES), BF16),
            pltpu.VMEM((maps, tile, LANES), F32),
            pltpu.VMEM((maps, tile, dv + LANES), F32),
        ],
        compiler_params=_params("parallel", "parallel", "arbitrary"),
        name="diff_attention",
    )(proj, proj, proj, bias_tiles, lam_params, head_gain)


def _ret_log_gamma(h):
    return math.log(1.0 - 2.0 ** (-5.0 - h))


def _retention_kernel(q_ref, k_ref, v_ref, g_ref, qcos_ref, qsin_ref, kcos_ref, ksin_ref,
                      o_ref, state_scr, *, ts):
    @pl.when(pl.program_id(1) == 0)
    def _():
        state_scr[...] = jnp.zeros(state_scr.shape, F32)

    c = RET_CHUNK
    d = RET_DIM
    row = lax.broadcasted_iota(jnp.int32, (c, c), 0).astype(F32)
    col = lax.broadcasted_iota(jnp.int32, (c, c), 1).astype(F32)
    rel = row - col
    qcos, qsin = qcos_ref[...], qsin_ref[...]
    kcos, ksin = kcos_ref[...], ksin_ref[...]

    for h in range(RET_HEADS):
        lg = _ret_log_gamma(h)
        intra = jnp.where(rel >= 0, jnp.exp(jnp.maximum(rel, 0.0) * lg), 0.0)
        key_decay = jnp.exp((c - 1 - row) * lg)
        query_decay = jnp.exp((row + 1) * lg)
        chunk_decay = math.exp(c * lg)
        cols = slice(h * d, (h + 1) * d)
        qh = q_ref[:, cols].astype(F32)
        kh = k_ref[:, cols].astype(F32)
        qh = qh * qcos + pltpu.roll(qh, d // 2, axis=1) * qsin
        kh = kh * kcos + pltpu.roll(kh, d // 2, axis=1) * ksin
        for ci in range(ts // c):
            rows = slice(ci * c, (ci + 1) * c)
            qc = qh[rows].astype(BF16)
            kc = kh[rows]
            vc = v_ref[rows, cols]
            state = state_scr[h]
            scores = lax.dot_general(qc, kc.astype(BF16), (((1,), (1,)), ((), ())),
                                     preferred_element_type=F32) * intra
            inner = jnp.dot(scores.astype(BF16), vc, preferred_element_type=F32)
            cross = jnp.dot(qc, state.astype(BF16), preferred_element_type=F32) * query_decay
            kv = lax.dot_general((kc * key_decay).astype(BF16), vc, (((0,), (0,)), ((), ())),
                                 preferred_element_type=F32)
            state_scr[h] = chunk_decay * state + kv
            out = _rmsnorm_rows(inner + cross)
            gate = g_ref[rows, cols].astype(F32)
            o_ref[rows, cols] = (gate * jax.nn.sigmoid(gate) * out).astype(o_ref.dtype)


def _rotary_tables(seq):
    half = RET_DIM // 2
    inv_freq = 1.0 / (ROPE_BASE ** jnp.linspace(0.0, 1.0, half, dtype=F32))
    ang = jnp.arange(seq, dtype=F32)[:, None] * inv_freq[None, :]
    cos, sin = jnp.cos(ang), jnp.sin(ang)
    cos2 = jnp.concatenate([cos, cos], axis=-1)
    sin2 = jnp.concatenate([-sin, sin], axis=-1)
    kscale = RET_DIM ** -0.5
    return cos2, sin2, cos2 * kscale, sin2 * kscale


def _retention(proj, tables, *, batch, seq, mix, q_col, ts):
    n = batch * seq
    spb = seq // ts
    cb = q_col // mix
    act = lambda off: pl.BlockSpec((ts, mix), lambda b, s: (b * spb + s, cb + off))
    tab = pl.BlockSpec((ts, RET_DIM), lambda b, s: (s, 0))
    return pl.pallas_call(
        functools.partial(_retention_kernel, ts=ts),
        grid=(batch, spb),
        in_specs=[act(0), act(1), act(2), act(3), tab, tab, tab, tab],
        out_specs=pl.BlockSpec((ts, mix), lambda b, s: (b * spb + s, 0)),
        out_shape=jax.ShapeDtypeStruct((n, mix), BF16),
        scratch_shapes=[pltpu.VMEM((RET_HEADS, RET_DIM, RET_DIM), F32)],
        compiler_params=_params("parallel", "arbitrary"),
        name="retention",
    )(proj, proj, proj, proj, *tables)


def _merge_kernel(yp_ref, yd_ref, yr_ref, g0_ref, g1_ref, g2_ref, w_ref, o_ref):
    acc = None
    for n, (y_ref, g_ref) in enumerate(((yp_ref, g0_ref), (yd_ref, g1_ref), (yr_ref, g2_ref))):
        wide = jnp.dot(y_ref[...], w_ref[n], preferred_element_type=F32)
        term = jax.nn.sigmoid(g_ref[...].astype(F32)) * wide
        acc = term if acc is None else acc + term
    o_ref[...] = acc.astype(o_ref.dtype)


def _merge(y_pool, y_diff, y_ret, proj, w_branch, layer, *, gate_col, tm, tn):
    n, mix = y_pool.shape
    d = w_branch.shape[3]
    y_spec = pl.BlockSpec((tm, mix), lambda i, j: (i, 0))
    gate = lambda br: pl.BlockSpec((tm, tn), lambda i, j: (i, (gate_col + br * d) // tn + j))
    return pl.pallas_call(
        _merge_kernel,
        grid=(n // tm, d // tn),
        in_specs=[y_spec, y_spec, y_spec, gate(0), gate(1), gate(2),
                  pl.BlockSpec((None, N_BRANCHES, mix, tn), lambda i, j: (layer, 0, 0, j))],
        out_specs=pl.BlockSpec((tm, tn), lambda i, j: (i, j)),
        out_shape=jax.ShapeDtypeStruct((n, d), BF16),
        compiler_params=_params("parallel", "arbitrary"),
        name="merge",
    )(y_pool, y_diff, y_ret, proj, proj, proj, w_branch)


def _outproj_kernel(m_ref, w_ref, x_ref, g_ref, o_ref):
    mixed = jnp.dot(m_ref[...], w_ref[...], preferred_element_type=F32)
    o_ref[...] = x_ref[...] + _rmsnorm_rows(mixed, g_ref[...])


def _outproj(merged, w_out, x, gain, layer, *, tm):
    n, d = x.shape
    row = pl.BlockSpec((tm, d), lambda i: (i, 0))
    return pl.pallas_call(
        _outproj_kernel,
        grid=(n // tm,),
        in_specs=[row, pl.BlockSpec((None, d, d), lambda i: (layer, 0, 0)), row,
                  pl.BlockSpec((1, d), lambda i: (0, 0))],
        out_specs=row,
        out_shape=jax.ShapeDtypeStruct((n, d), F32),
        compiler_params=_params("parallel"),
        name="outproj",
    )(merged, w_out, x, gain)


def _mlp_kernel(x_ref, gpre_ref, wup_ref, wdown_ref, gpost_ref, o_ref, h_scr, acc_scr):
    k = pl.program_id(1)

    @pl.when(k == 0)
    def _():
        h_scr[...] = _rmsnorm_rows(x_ref[...], gpre_ref[...]).astype(BF16)
        acc_scr[...] = jnp.zeros(acc_scr.shape, F32)

    up = jnp.dot(h_scr[...], wup_ref[...], preferred_element_type=F32)
    ff = jnp.square(jnp.maximum(up, 0.0)).astype(BF16)
    acc_scr[...] += jnp.dot(ff, wdown_ref[...], preferred_element_type=F32)

    @pl.when(k == pl.num_programs(1) - 1)
    def _():
        o_ref[...] = x_ref[...] + _rmsnorm_rows(acc_scr[...], gpost_ref[...])


def _mlp(x, gain_pre, w_up, w_down, gain_post, layer, *, tm, tf):
    n, d = x.shape
    f = w_up.shape[2]
    row = pl.BlockSpec((tm, d), lambda i, k: (i, 0))
    vec = pl.BlockSpec((1, d), lambda i, k: (0, 0))
    return pl.pallas_call(
        _mlp_kernel,
        grid=(n // tm, f // tf),
        in_specs=[row, vec, pl.BlockSpec((None, d, tf), lambda i, k: (layer, 0, k)),
                  pl.BlockSpec((None, tf, d), lambda i, k: (layer, k, 0)), vec],
        out_specs=row,
        out_shape=jax.ShapeDtypeStruct((n, d), F32),
        scratch_shapes=[pltpu.VMEM((tm, d), BF16), pltpu.VMEM((tm, d), F32)],
        compiler_params=_params("parallel", "arbitrary"),
        name="mlp",
    )(x, gain_pre, w_up, w_down, gain_post)


def _tiles(n_tokens, seq, in_cols, d_model, d_ff):
    pick = lambda total, want: want if total % want == 0 else total
    return dict(
        inproj=dict(tm=pick(n_tokens, 1024), tn=pick(in_cols, 1024)),
        pool=dict(ts=pick(seq, 512)),
        diff=dict(tile=pick(seq, 512)),
        ret=dict(ts=pick(seq, 512)),
        merge=dict(tm=pick(n_tokens, 512), tn=d_model),
        outproj=dict(tm=pick(n_tokens, 512)),
        mlp=dict(tm=pick(n_tokens, 512), tf=pick(d_ff, 1024)),
    )


def kernel(x, rel_bias, norm_pre_mix, norm_post_mix, norm_pre_mlp, norm_post_mlp, w_in, pool_w,
           pool_scale, diff_lambda, diff_head_norm, w_branch, w_out, w_up, w_down):
    batch, seq, d_model = x.shape
    depth, _, in_cols = w_in.shape
    mix = pool_scale.shape[1]
    d_ff = w_up.shape[2]
    n = batch * seq
    assert mix == DIFF_HEADS * DIFF_V_DIM == RET_HEADS * RET_DIM
    assert in_cols == 8 * mix + N_BRANCHES * d_model
    tiles = _tiles(n, seq, in_cols, d_model, d_ff)

    col = lambda idx: idx * mix
    bias_tiles = _bias_tiles(rel_bias, tile=tiles["diff"]["tile"])
    tables = _rotary_tables(seq)
    vec = lambda a: a.astype(F32).reshape(1, -1)
    w_in, pool_w, w_branch, w_out, w_up, w_down = (
        a.astype(BF16) for a in (w_in, pool_w, w_branch, w_out, w_up, w_down))

    xs = x.astype(F32).reshape(n, d_model)
    for l in range(depth):
        lambda_init = 0.8 - 0.6 * math.exp(-0.3 * l)
        proj = _inproj(xs, vec(norm_pre_mix[l]), w_in, l, **tiles["inproj"])
        y_pool = _pool(proj, pool_w, vec(pool_scale[l]), l,
                       batch=batch, seq=seq, mix=mix, **tiles["pool"])
        y_diff = _diff_attention(proj, bias_tiles, diff_lambda[l].astype(F32),
                                 vec(diff_head_norm[l]), batch=batch, seq=seq,
                                 q_col=col(1), k_col=col(2), v_col=col(3),
                                 lambda_init=lambda_init, **tiles["diff"])
        y_ret = _retention(proj, tables, batch=batch, seq=seq, mix=mix, q_col=col(4),
                           **tiles["ret"])
        merged = _merge(y_pool, y_diff, y_ret, proj, w_branch, l,
                        gate_col=col(8), **tiles["merge"])
        xs = _outproj(merged, w_out, xs, vec(norm_post_mix[l]), l, **tiles["outproj"])
        xs = _mlp(xs, vec(norm_pre_mlp[l]), w_up, w_down, vec(norm_post_mlp[l]), l,
                  **tiles["mlp"])
    return xs.reshape(batch, seq, d_model).astype(x.dtype)
```

```python
import functools
import math

import jax
import jax.numpy as jnp
from jax import lax
from jax.experimental import pallas as pl
from jax.experimental.pallas import tpu as pltpu

F32 = jnp.float32
BF16 = jnp.bfloat16

LANES = 128
POOL_WINDOWS = (2, 4, 8, 16)
POOL_HALO = 16
DIFF_HEADS = 8
DIFF_QK_DIM = 64
DIFF_V_DIM = 128
DIFF_HEADS_PER_STEP = 2
RET_HEADS = 8
RET_DIM = 128
RET_CHUNK = 128
ROPE_BASE = 10000.0
N_BRANCHES = 3
REL_BUCKETS = 32
REL_MAX_DIST = 128
NORM_EPS = 1e-6
NORM_OVERLAP_ROWS = 256
NEG_INF = -1e30
VMEM_LIMIT = 56 * 1024 * 1024


def _params(*semantics):
    return pltpu.CompilerParams(dimension_semantics=semantics, vmem_limit_bytes=VMEM_LIMIT)


def _rmsnorm_rows(x, gain=None):
    y = x * lax.rsqrt(jnp.mean(x * x, axis=-1, keepdims=True) + NORM_EPS)
    return y if gain is None else y * gain


def _row_chunks(total, size=NORM_OVERLAP_ROWS):
    return [slice(r, r + size) for r in range(0, total, size)]


def _inproj_kernel(x_ref, g_ref, w_ref, o_ref, a_scr):
    j = pl.program_id(1)

    def project(rows):
        o_ref[rows] = jnp.dot(a_scr[rows], w_ref[...],
                              preferred_element_type=F32).astype(o_ref.dtype)

    @pl.when(j == 0)
    def _():
        for rows in _row_chunks(x_ref.shape[0]):
            a_scr[rows] = _rmsnorm_rows(x_ref[rows], g_ref[...]).astype(BF16)
            project(rows)

    @pl.when(j > 0)
    def _():
        project(slice(None))


def _inproj(x, gain, w, layer, *, tm, tn):
    n, d = x.shape
    c = w.shape[2]
    return pl.pallas_call(
        _inproj_kernel,
        grid=(n // tm, c // tn),
        in_specs=[
            pl.BlockSpec((tm, d), lambda i, j: (i, 0)),
            pl.BlockSpec((1, d), lambda i, j: (0, 0)),
            pl.BlockSpec((None, d, tn), lambda i, j: (layer, 0, j)),
        ],
        out_specs=pl.BlockSpec((tm, tn), lambda i, j: (i, j)),
        out_shape=jax.ShapeDtypeStruct((n, c), BF16),
        scratch_shapes=[pltpu.VMEM((tm, d), BF16)],
        compiler_params=_params("parallel", "arbitrary"),
        name="inproj",
    )(x, gain, w)


def _pool_kernel(u_ref, halo_ref, w_ref, scale_ref, o_ref, ext_scr, *, ts):
    s = pl.program_id(1)
    mix = u_ref.shape[1]
    gdim = mix // len(POOL_WINDOWS)
    cur = u_ref[...].astype(F32)
    ext_scr[0:POOL_HALO, :] = jnp.where(s > 0, halo_ref[...].astype(F32), 0.0)
    ext_scr[POOL_HALO:, :] = cur
    t = s * ts + lax.broadcasted_iota(jnp.int32, (ts, 1), 0)
    for g, win in enumerate(POOL_WINDOWS):
        cols = slice(g * gdim, (g + 1) * gdim)
        tot = ext_scr[POOL_HALO:POOL_HALO + ts, cols]
        for j in range(1, win):
            tot = tot + ext_scr[POOL_HALO - j:POOL_HALO - j + ts, cols]
        count = jnp.minimum(t + 1, win).astype(F32)
        delta = tot / count - cur[:, cols]
        y = jnp.dot(delta.astype(BF16), w_ref[g], preferred_element_type=F32)
        o_ref[:, cols] = (y * scale_ref[:, cols]).astype(o_ref.dtype)


def _pool(proj, pool_w, pool_scale, layer, *, batch, seq, mix, ts):
    n = batch * seq
    spb = seq // ts
    hpt = ts // POOL_HALO
    g = len(POOL_WINDOWS)
    return pl.pallas_call(
        functools.partial(_pool_kernel, ts=ts),
        grid=(batch, spb),
        in_specs=[
            pl.BlockSpec((ts, mix), lambda b, s: (b * spb + s, 0)),
            pl.BlockSpec((POOL_HALO, mix),
                         lambda b, s: (jnp.maximum((b * spb + s) * hpt - 1, 0), 0)),
            pl.BlockSpec((None, g, mix // g, mix // g), lambda b, s: (layer, 0, 0, 0)),
            pl.BlockSpec((1, mix), lambda b, s: (0, 0)),
        ],
        out_specs=pl.BlockSpec((ts, mix), lambda b, s: (b * spb + s, 0)),
        out_shape=jax.ShapeDtypeStruct((n, mix), BF16),
        scratch_shapes=[pltpu.VMEM((ts + POOL_HALO, mix), F32)],
        compiler_params=_params("parallel", "arbitrary"),
        name="pool",
    )(proj, proj, pool_w, pool_scale)


def _bias_tiles_kernel(tbl_ref, o_ref, *, tile):
    h = pl.program_id(0)
    exact = REL_BUCKETS // 2
    row = lax.broadcasted_iota(jnp.int32, (tile, tile), 0)
    col = lax.broadcasted_iota(jnp.int32, (tile, tile), 1)
    far = tbl_ref[h * REL_BUCKETS + REL_BUCKETS - 1]
    for which in range(2):
        dist = row - col + which * tile
        n = jnp.maximum(dist, 0)
        nf = jnp.maximum(n, 1).astype(F32)
        large = exact + (jnp.log(nf / exact) / math.log(REL_MAX_DIST / exact)
                         * (REL_BUCKETS - exact)).astype(jnp.int32)
        large = jnp.minimum(large, REL_BUCKETS - 1)
        bucket = jnp.where(n < exact, n, large)
        bias = jnp.zeros((tile, tile), F32)
        for b in range(REL_BUCKETS):
            bias = jnp.where(bucket == b, tbl_ref[h * REL_BUCKETS + b], bias)
        bias = bias - far
        if which == 0:
            bias = jnp.where(dist >= 0, bias, NEG_INF)
        o_ref[0, :, (1 - which) * tile:(2 - which) * tile] = bias


def _bias_tiles(rel_bias, *, tile):
    heads = rel_bias.shape[1]
    tbl = rel_bias.astype(F32).T.reshape(-1)
    return pl.pallas_call(
        functools.partial(_bias_tiles_kernel, tile=tile),
        grid=(heads,),
        in_specs=[pl.BlockSpec(memory_space=pltpu.SMEM)],
        out_specs=pl.BlockSpec((1, tile, 2 * tile), lambda h: (h, 0, 0)),
        out_shape=jax.ShapeDtypeStruct((heads, tile, 2 * tile), F32),
        compiler_params=_params("arbitrary"),
        name="bias_tiles",
    )(tbl)


def _diff_kernel(q_ref, k_ref, v_ref, bias_ref, lam_ref, gain_ref, o_ref,
                 kt_scr, v1_scr, m_scr, acc_scr, *, tile, lambda_init):
    i = pl.program_id(2)
    seq = k_ref.shape[0]
    dq = DIFF_QK_DIM
    dv = DIFF_V_DIM
    heads = range(DIFF_HEADS_PER_STEP)

    @pl.when(i == 0)
    def _():
        for hh in heads:
            for c in range(seq // tile):
                rows = slice(c * tile, (c + 1) * tile)
                kt_scr[hh, :, rows] = (
                    k_ref[rows, hh * dv:(hh + 1) * dv].astype(F32).T.astype(BF16))
            v1_scr[hh, :, :dv] = v_ref[:, hh * dv:(hh + 1) * dv]
            v1_scr[hh, :, dv:] = jnp.ones((seq, LANES), BF16)

    m_scr[...] = jnp.full(m_scr.shape, NEG_INF, F32)
    acc_scr[...] = jnp.zeros(acc_scr.shape, F32)

    q = q_ref[...] * (dq ** -0.5)
    qs = [q[:, c * dq:(c + 1) * dq] for c in range(2 * DIFF_HEADS_PER_STEP)]

    def tile_update(start, width, biased):
        scores = [jnp.dot(qs[2 * hh + m], kt_scr[hh, m * dq:(m + 1) * dq, pl.ds(start, width)],
                          preferred_element_type=F32) for hh in heads for m in range(2)]
        for hh in heads:
            v1 = v1_scr[hh, pl.ds(start, width), :]
            for m in range(2):
                c = 2 * hh + m
                s = scores[c]
                if biased:
                    s = bias_ref[hh, :, 2 * tile - width:] + s
                m_prev = m_scr[c]
                m_next = jnp.maximum(m_prev, jnp.max(s, axis=1, keepdims=True))
                p = jnp.exp(s - jnp.tile(m_next, (1, width // LANES))).astype(BF16)
                alpha = jnp.exp(m_prev - m_next)
                m_scr[c] = m_next
                acc_scr[c] = jnp.tile(alpha, (1, 2)) * acc_scr[c] + jnp.dot(
                    p, v1, preferred_element_type=F32)

    n_far = jnp.maximum(i - 1, 0)

    def far_body(j, carry):
        tile_update(pl.multiple_of(j * (2 * tile), 2 * tile), 2 * tile, False)
        return carry

    lax.fori_loop(0, n_far // 2, far_body, 0)

    @pl.when(n_far % 2 == 1)
    def _():
        tile_update(pl.multiple_of((n_far - 1) * tile, tile), tile, False)

    @pl.when(i > 0)
    def _():
        tile_update(pl.multiple_of((i - 1) * tile, tile), 2 * tile, True)

    @pl.when(i == 0)
    def _():
        tile_update(0, tile, True)

    lp = lam_ref[...]
    lam = (jnp.exp(jnp.sum(lp[0:1] * lp[1:2], axis=-1, keepdims=True))
           - jnp.exp(jnp.sum(lp[2:3] * lp[3:4], axis=-1, keepdims=True)) + lambda_init)
    for hh in heads:
        c0, c1 = 2 * hh, 2 * hh + 1
        out = (acc_scr[c0, :, :dv] / acc_scr[c0, :, dv:]
               - lam * (acc_scr[c1, :, :dv] / acc_scr[c1, :, dv:]))
        out = _rmsnorm_rows(out, gain_ref[...]) * (1.0 - lambda_init)
        o_ref[:, hh * dv:(hh + 1) * dv] = out.astype(o_ref.dtype)


def _diff_attention(proj, bias_tiles, lam_params, head_gain, *, batch, seq, q_col, k_col, v_col,
                    tile, lambda_init):
    n = batch * seq
    qpb = seq // tile
    hps = DIFF_HEADS_PER_STEP
    dv = DIFF_V_DIM
    gw = hps * dv
    qb, kb, vb = q_col // gw, k_col // gw, v_col // gw
    maps = 2 * hps
    return pl.pallas_call(
        functools.partial(_diff_kernel, tile=tile, lambda_init=lambda_init),
        grid=(batch, DIFF_HEADS // hps, qpb),
        in_specs=[
            pl.BlockSpec((tile, gw), lambda b, g, i: (b * qpb + i, qb + g)),
            pl.BlockSpec((seq, gw), lambda b, g, i: (b, kb + g)),
            pl.BlockSpec((seq, gw), lambda b, g, i: (b, vb + g)),
            pl.BlockSpec((hps, tile, 2 * tile), lambda b, g, i: (g, 0, 0)),
            pl.BlockSpec(lam_params.shape, lambda b, g, i: (0, 0)),
            pl.BlockSpec((1, dv), lambda b, g, i: (0, 0)),
        ],
        out_specs=pl.BlockSpec((tile, gw), lambda b, g, i: (b * qpb + i, g)),
        out_shape=jax.ShapeDtypeStruct((n, DIFF_HEADS * dv), BF16),
        scratch_shapes=[
            pltpu.VMEM((hps, dv, seq), BF16),
            pltpu.VMEM((hps, seq, dv + LANES), BF16),
            pltpu.VMEM((maps, tile, LANES), F32),
            pltpu.VMEM((maps, tile, dv + LANES), F32),
        ],
        compiler_params=_params("parallel", "parallel", "arbitrary"),
        name="diff_attention",
    )(proj, proj, proj, bias_tiles, lam_params, head_gain)


def _ret_log_gamma(h):
    return math.log(1.0 - 2.0 ** (-5.0 - h))


def _retention_kernel(q_ref, k_ref, v_ref, g_ref, qcos_ref, qsin_ref, kcos_ref, ksin_ref,
                      o_ref, state_scr, decay_scr, *, ts):
    @pl.when(pl.program_id(1) == 0)
    def _():
        state_scr[...] = jnp.zeros(state_scr.shape, F32)

    c = RET_CHUNK
    d = RET_DIM
    row = lax.broadcasted_iota(jnp.int32, (c, c), 0).astype(F32)
    col = lax.broadcasted_iota(jnp.int32, (c, c), 1).astype(F32)
    rel = row - col
    heads = range(RET_HEADS)
    col_of = [slice(h * d, (h + 1) * d) for h in heads]
    log_gamma = [_ret_log_gamma(h) for h in heads]
    chunk_decay = [math.exp(c * lg) for lg in log_gamma]

    @pl.when(pl.program_id(1) == 0)
    def _():
        for h, lg in enumerate(log_gamma):
            decay_scr[0, h] = jnp.where(rel >= 0, jnp.exp(jnp.maximum(rel, 0.0) * lg), 0.0)
            decay_scr[1, h] = jnp.exp((c - 1 - row) * lg)
            decay_scr[2, h] = jnp.exp((row + 1) * lg)

    intra, key_decay, query_decay = (decay_scr.at[t] for t in range(3))
    nt_dims = (((1,), (1,)), ((), ()))
    tn_dims = (((0,), (0,)), ((), ()))

    states = [state_scr[h] for h in heads]
    for ci in range(ts // c):
        rows = slice(ci * c, (ci + 1) * c)
        qcos, qsin = qcos_ref[rows, :], qsin_ref[rows, :]
        kcos, ksin = kcos_ref[rows, :], ksin_ref[rows, :]
        qs, ks, kds, vs = [], [], [], []
        for h in heads:
            qh = q_ref[rows, col_of[h]].astype(F32)
            kh = k_ref[rows, col_of[h]].astype(F32)
            qh = qh * qcos + pltpu.roll(qh, d // 2, axis=1) * qsin
            kh = kh * kcos + pltpu.roll(kh, d // 2, axis=1) * ksin
            qs.append(qh.astype(BF16))
            ks.append(kh.astype(BF16))
            kds.append((kh * key_decay[h]).astype(BF16))
            vs.append(v_ref[rows, col_of[h]])
        scores = [lax.dot_general(qs[h], ks[h], nt_dims, preferred_element_type=F32)
                  for h in heads]
        cross = [jnp.dot(qs[h], states[h].astype(BF16), preferred_element_type=F32)
                 for h in heads]
        kv = [lax.dot_general(kds[h], vs[h], tn_dims, preferred_element_type=F32)
              for h in heads]
        inner = [jnp.dot((scores[h] * intra[h]).astype(BF16), vs[h], preferred_element_type=F32)
                 for h in heads]
        for h in heads:
            states[h] = chunk_decay[h] * states[h] + kv[h]
            out = _rmsnorm_rows(inner[h] + cross[h] * query_decay[h])
            gate = g_ref[rows, col_of[h]].astype(F32)
            o_ref[rows, col_of[h]] = (gate * jax.nn.sigmoid(gate) * out).astype(o_ref.dtype)
    for h in heads:
        state_scr[h] = states[h]


def _rotary_tables(seq):
    half = RET_DIM // 2
    inv_freq = 1.0 / (ROPE_BASE ** jnp.linspace(0.0, 1.0, half, dtype=F32))
    ang = jnp.arange(seq, dtype=F32)[:, None] * inv_freq[None, :]
    cos, sin = jnp.cos(ang), jnp.sin(ang)
    cos2 = jnp.concatenate([cos, cos], axis=-1)
    sin2 = jnp.concatenate([-sin, sin], axis=-1)
    kscale = RET_DIM ** -0.5
    return cos2, sin2, cos2 * kscale, sin2 * kscale


def _retention(proj, tables, *, batch, seq, mix, q_col, ts):
    n = batch * seq
    spb = seq // ts
    cb = q_col // mix
    act = lambda off: pl.BlockSpec((ts, mix), lambda b, s: (b * spb + s, cb + off))
    tab = pl.BlockSpec((ts, RET_DIM), lambda b, s: (s, 0))
    return pl.pallas_call(
        functools.partial(_retention_kernel, ts=ts),
        grid=(batch, spb),
        in_specs=[act(0), act(1), act(2), act(3), tab, tab, tab, tab],
        out_specs=pl.BlockSpec((ts, mix), lambda b, s: (b * spb + s, 0)),
        out_shape=jax.ShapeDtypeStruct((n, mix), BF16),
        scratch_shapes=[pltpu.VMEM((RET_HEADS, RET_DIM, RET_DIM), F32),
                        pltpu.VMEM((3, RET_HEADS, RET_CHUNK, RET_DIM), F32)],
        compiler_params=_params("parallel", "arbitrary"),
        name="retention",
    )(proj, proj, proj, proj, *tables)


def _merge_kernel(yp_ref, yd_ref, yr_ref, g0_ref, g1_ref, g2_ref, w_ref, o_ref):
    for rows in _row_chunks(o_ref.shape[0]):
        acc = None
        for n, (y_ref, g_ref) in enumerate(((yp_ref, g0_ref), (yd_ref, g1_ref),
                                            (yr_ref, g2_ref))):
            wide = jnp.dot(y_ref[rows], w_ref[n], preferred_element_type=F32)
            term = jax.nn.sigmoid(g_ref[rows].astype(F32)) * wide
            acc = term if acc is None else acc + term
        o_ref[rows] = acc.astype(o_ref.dtype)


def _merge(y_pool, y_diff, y_ret, proj, w_branch, layer, *, gate_col, tm, tn):
    n, mix = y_pool.shape
    d = w_branch.shape[3]
    y_spec = pl.BlockSpec((tm, mix), lambda i, j: (i, 0))
    gate = lambda br: pl.BlockSpec((tm, tn), lambda i, j: (i, (gate_col + br * d) // tn + j))
    return pl.pallas_call(
        _merge_kernel,
        grid=(n // tm, d // tn),
        in_specs=[y_spec, y_spec, y_spec, gate(0), gate(1), gate(2),
                  pl.BlockSpec((None, N_BRANCHES, mix, tn), lambda i, j: (layer, 0, 0, j))],
        out_specs=pl.BlockSpec((tm, tn), lambda i, j: (i, j)),
        out_shape=jax.ShapeDtypeStruct((n, d), BF16),
        compiler_params=_params("parallel", "arbitrary"),
        name="merge",
    )(y_pool, y_diff, y_ret, proj, proj, proj, w_branch)


def _outproj_kernel(m_ref, w_ref, x_ref, g_ref, o_ref):
    for rows in _row_chunks(x_ref.shape[0]):
        mixed = jnp.dot(m_ref[rows], w_ref[...], preferred_element_type=F32)
        o_ref[rows] = x_ref[rows] + _rmsnorm_rows(mixed, g_ref[...])


def _outproj(merged, w_out, x, gain, layer, *, tm):
    n, d = x.shape
    row = pl.BlockSpec((tm, d), lambda i: (i, 0))
    return pl.pallas_call(
        _outproj_kernel,
        grid=(n // tm,),
        in_specs=[row, pl.BlockSpec((None, d, d), lambda i: (layer, 0, 0)), row,
                  pl.BlockSpec((1, d), lambda i: (0, 0))],
        out_specs=row,
        out_shape=jax.ShapeDtypeStruct((n, d), F32),
        compiler_params=_params("parallel"),
        name="outproj",
    )(merged, w_out, x, gain)


def _mlp_kernel(x_ref, gpre_ref, wup_ref, wdown_ref, gpost_ref, o_ref, h_scr, acc_scr):
    k = pl.program_id(1)
    last = pl.num_programs(1) - 1
    chunks = _row_chunks(x_ref.shape[0])

    def ff_partial(rows):
        up = jnp.dot(h_scr[rows], wup_ref[...], preferred_element_type=F32)
        ff = jnp.square(jnp.maximum(up, 0.0)).astype(BF16)
        return jnp.dot(ff, wdown_ref[...], preferred_element_type=F32)

    @pl.when(k == 0)
    def _():
        for rows in chunks:
            h_scr[rows] = _rmsnorm_rows(x_ref[rows], gpre_ref[...]).astype(BF16)
            acc_scr[rows] = ff_partial(rows)

    @pl.when(jnp.logical_and(k > 0, k < last))
    def _():
        acc_scr[...] += ff_partial(slice(None))

    @pl.when(k == last)
    def _():
        for rows in chunks:
            ff = acc_scr[rows] + ff_partial(rows)
            o_ref[rows] = x_ref[rows] + _rmsnorm_rows(ff, gpost_ref[...])


def _mlp(x, gain_pre, w_up, w_down, gain_post, layer, *, tm, tf):
    n, d = x.shape
    f = w_up.shape[2]
    assert f // tf >= 2
    row = pl.BlockSpec((tm, d), lambda i, k: (i, 0))
    vec = pl.BlockSpec((1, d), lambda i, k: (0, 0))
    return pl.pallas_call(
        _mlp_kernel,
        grid=(n // tm, f // tf),
        in_specs=[row, vec, pl.BlockSpec((None, d, tf), lambda i, k: (layer, 0, k)),
                  pl.BlockSpec((None, tf, d), lambda i, k: (layer, k, 0)), vec],
        out_specs=row,
        out_shape=jax.ShapeDtypeStruct((n, d), F32),
        scratch_shapes=[pltpu.VMEM((tm, d), BF16), pltpu.VMEM((tm, d), F32)],
        compiler_params=_params("parallel", "arbitrary"),
        name="mlp",
    )(x, gain_pre, w_up, w_down, gain_post)


def _tiles(n_tokens, seq, in_cols, d_model, d_ff):
    pick = lambda total, want: want if total % want == 0 else total
    return dict(
        inproj=dict(tm=pick(n_tokens, 1024), tn=pick(in_cols, 2048)),
        pool=dict(ts=pick(seq, 512)),
        diff=dict(tile=pick(seq, 512)),
        ret=dict(ts=pick(seq, 512)),
        merge=dict(tm=pick(n_tokens, 512), tn=d_model),
        outproj=dict(tm=pick(n_tokens, 512)),
        mlp=dict(tm=pick(n_tokens, 512), tf=pick(d_ff, 1024)),
    )


def kernel(x, rel_bias, norm_pre_mix, norm_post_mix, norm_pre_mlp, norm_post_mlp, w_in, pool_w,
           pool_scale, diff_lambda, diff_head_norm, w_branch, w_out, w_up, w_down):
    batch, seq, d_model = x.shape
    depth, _, in_cols = w_in.shape
    mix = pool_scale.shape[1]
    d_ff = w_up.shape[2]
    n = batch * seq
    assert mix == DIFF_HEADS * DIFF_V_DIM == RET_HEADS * RET_DIM
    assert in_cols == 8 * mix + N_BRANCHES * d_model
    tiles = _tiles(n, seq, in_cols, d_model, d_ff)

    col = lambda idx: idx * mix
    bias_tiles = _bias_tiles(rel_bias, tile=tiles["diff"]["tile"])
    tables = _rotary_tables(seq)
    vec = lambda a: a.astype(F32).reshape(1, -1)
    w_in, pool_w, w_branch, w_out, w_up, w_down = (
        a.astype(BF16) for a in (w_in, pool_w, w_branch, w_out, w_up, w_down))

    xs = x.astype(F32).reshape(n, d_model)
    for l in range(depth):
        lambda_init = 0.8 - 0.6 * math.exp(-0.3 * l)
        proj = _inproj(xs, vec(norm_pre_mix[l]), w_in, l, **tiles["inproj"])
        y_pool = _pool(proj, pool_w, vec(pool_scale[l]), l,
                       batch=batch, seq=seq, mix=mix, **tiles["pool"])
        y_diff = _diff_attention(proj, bias_tiles, diff_lambda[l].astype(F32),
                                 vec(diff_head_norm[l]), batch=batch, seq=seq,
                                 q_col=col(1), k_col=col(2), v_col=col(3),
                                 lambda_init=lambda_init, **tiles["diff"])
        y_ret = _retention(proj, tables, batch=batch, seq=seq, mix=mix, q_col=col(4),
                           **tiles["ret"])
        merged = _merge(y_pool, y_diff, y_ret, proj, w_branch, l,
                        gate_col=col(8), **tiles["merge"])
        xs = _outproj(merged, w_out, xs, vec(norm_post_mix[l]), l, **tiles["outproj"])
        xs = _mlp(xs, vec(norm_pre_mlp[l]), w_up, w_down, vec(norm_post_mlp[l]), l,
                  **tiles["mlp"])
    return xs.reshape(batch, seq, d_model).astype(x.dtype)
```

```python
import functools
import math

import jax
import jax.numpy as jnp
from jax import lax
from jax.experimental import pallas as pl
from jax.experimental.pallas import tpu as pltpu

F32 = jnp.float32
BF16 = jnp.bfloat16

LANES = 128
POOL_WINDOWS = (2, 4, 8, 16)
POOL_HALO = 16
DIFF_HEADS = 8
DIFF_QK_DIM = 64
DIFF_V_DIM = 128
DIFF_HEADS_PER_STEP = 2
RET_HEADS = 8
RET_DIM = 128
RET_CHUNK = 128
ROPE_BASE = 10000.0
N_BRANCHES = 3
REL_BUCKETS = 32
REL_MAX_DIST = 128
NORM_EPS = 1e-6
NORM_OVERLAP_ROWS = 256
NEG_INF = -1e30
VMEM_LIMIT = 56 * 1024 * 1024


def _params(*semantics):
    return pltpu.CompilerParams(dimension_semantics=semantics, vmem_limit_bytes=VMEM_LIMIT)


def _rmsnorm_rows(x, gain=None):
    y = x * lax.rsqrt(jnp.mean(x * x, axis=-1, keepdims=True) + NORM_EPS)
    return y if gain is None else y * gain


def _row_chunks(total, size=NORM_OVERLAP_ROWS):
    return [slice(r, r + size) for r in range(0, total, size)]


def _inproj_kernel(x_ref, g_ref, w_ref, o_ref, a_scr):
    j = pl.program_id(1)

    def project(rows):
        o_ref[rows] = jnp.dot(a_scr[rows], w_ref[...],
                              preferred_element_type=F32).astype(o_ref.dtype)

    @pl.when(j == 0)
    def _():
        for rows in _row_chunks(x_ref.shape[0]):
            a_scr[rows] = _rmsnorm_rows(x_ref[rows], g_ref[...]).astype(BF16)
            project(rows)

    @pl.when(j > 0)
    def _():
        project(slice(None))


def _inproj(x, gain, w, layer, *, tm, tn):
    n, d = x.shape
    c = w.shape[2]
    return pl.pallas_call(
        _inproj_kernel,
        grid=(n // tm, c // tn),
        in_specs=[
            pl.BlockSpec((tm, d), lambda i, j: (i, 0)),
            pl.BlockSpec((1, d), lambda i, j: (0, 0)),
            pl.BlockSpec((None, d, tn), lambda i, j: (layer, 0, j)),
        ],
        out_specs=pl.BlockSpec((tm, tn), lambda i, j: (i, j)),
        out_shape=jax.ShapeDtypeStruct((n, c), BF16),
        scratch_shapes=[pltpu.VMEM((tm, d), BF16)],
        compiler_params=_params("parallel", "arbitrary"),
        name="inproj",
    )(x, gain, w)


def _pool_kernel(u_ref, halo_ref, w_ref, scale_ref, o_ref, ext_scr, *, ts):
    s = pl.program_id(1)
    mix = u_ref.shape[1]
    gdim = mix // len(POOL_WINDOWS)
    cur = u_ref[...].astype(F32)
    ext_scr[0:POOL_HALO, :] = jnp.where(s > 0, halo_ref[...].astype(F32), 0.0)
    ext_scr[POOL_HALO:, :] = cur
    t = s * ts + lax.broadcasted_iota(jnp.int32, (ts, 1), 0)
    for g, win in enumerate(POOL_WINDOWS):
        cols = slice(g * gdim, (g + 1) * gdim)
        tot = ext_scr[POOL_HALO:POOL_HALO + ts, cols]
        for j in range(1, win):
            tot = tot + ext_scr[POOL_HALO - j:POOL_HALO - j + ts, cols]
        count = jnp.minimum(t + 1, win).astype(F32)
        delta = tot / count - cur[:, cols]
        y = jnp.dot(delta.astype(BF16), w_ref[g], preferred_element_type=F32)
        o_ref[:, cols] = (y * scale_ref[:, cols]).astype(o_ref.dtype)


def _pool(proj, pool_w, pool_scale, layer, *, batch, seq, mix, ts):
    n = batch * seq
    spb = seq // ts
    hpt = ts // POOL_HALO
    g = len(POOL_WINDOWS)
    return pl.pallas_call(
        functools.partial(_pool_kernel, ts=ts),
        grid=(batch, spb),
        in_specs=[
            pl.BlockSpec((ts, mix), lambda b, s: (b * spb + s, 0)),
            pl.BlockSpec((POOL_HALO, mix),
                         lambda b, s: (jnp.maximum((b * spb + s) * hpt - 1, 0), 0)),
            pl.BlockSpec((None, g, mix // g, mix // g), lambda b, s: (layer, 0, 0, 0)),
            pl.BlockSpec((1, mix), lambda b, s: (0, 0)),
        ],
        out_specs=pl.BlockSpec((ts, mix), lambda b, s: (b * spb + s, 0)),
        out_shape=jax.ShapeDtypeStruct((n, mix), BF16),
        scratch_shapes=[pltpu.VMEM((ts + POOL_HALO, mix), F32)],
        compiler_params=_params("parallel", "arbitrary"),
        name="pool",
    )(proj, proj, pool_w, pool_scale)


def _bias_tiles_kernel(tbl_ref, o_ref, *, tile):
    h = pl.program_id(0)
    exact = REL_BUCKETS // 2
    row = lax.broadcasted_iota(jnp.int32, (tile, tile), 0)
    col = lax.broadcasted_iota(jnp.int32, (tile, tile), 1)
    far = tbl_ref[h * REL_BUCKETS + REL_BUCKETS - 1]
    for which in range(2):
        dist = row - col + which * tile
        n = jnp.maximum(dist, 0)
        nf = jnp.maximum(n, 1).astype(F32)
        large = exact + (jnp.log(nf / exact) / math.log(REL_MAX_DIST / exact)
                         * (REL_BUCKETS - exact)).astype(jnp.int32)
        large = jnp.minimum(large, REL_BUCKETS - 1)
        bucket = jnp.where(n < exact, n, large)
        bias = jnp.zeros((tile, tile), F32)
        for b in range(REL_BUCKETS):
            bias = jnp.where(bucket == b, tbl_ref[h * REL_BUCKETS + b], bias)
        bias = bias - far
        if which == 0:
            bias = jnp.where(dist >= 0, bias, NEG_INF)
        o_ref[0, :, (1 - which) * tile:(2 - which) * tile] = bias


def _bias_tiles(rel_bias, *, tile):
    heads = rel_bias.shape[1]
    tbl = rel_bias.astype(F32).T.reshape(-1)
    return pl.pallas_call(
        functools.partial(_bias_tiles_kernel, tile=tile),
        grid=(heads,),
        in_specs=[pl.BlockSpec(memory_space=pltpu.SMEM)],
        out_specs=pl.BlockSpec((1, tile, 2 * tile), lambda h: (h, 0, 0)),
        out_shape=jax.ShapeDtypeStruct((heads, tile, 2 * tile), F32),
        compiler_params=_params("arbitrary"),
        name="bias_tiles",
    )(tbl)


def _diff_kernel(q_ref, k_ref, v_ref, bias_ref, lam_ref, gain_ref, o_ref,
                 kt_scr, v1_scr, m_scr, acc_scr, *, tile, lambda_init):
    seq = k_ref.shape[0]
    dq = DIFF_QK_DIM
    dv = DIFF_V_DIM
    heads = range(DIFF_HEADS_PER_STEP)

    for hh in heads:
        for c in range(seq // tile):
            rows = slice(c * tile, (c + 1) * tile)
            kt_scr[hh, :, rows] = k_ref[rows, hh * dv:(hh + 1) * dv].astype(F32).T.astype(BF16)
        v1_scr[hh, :, :dv] = v_ref[:, hh * dv:(hh + 1) * dv]
        v1_scr[hh, :, dv:] = jnp.ones((seq, LANES), BF16)

    lp = lam_ref[...]
    lam = (jnp.exp(jnp.sum(lp[0:1] * lp[1:2], axis=-1, keepdims=True))
           - jnp.exp(jnp.sum(lp[2:3] * lp[3:4], axis=-1, keepdims=True)) + lambda_init)

    def tile_update(qs, start, width, biased):
        scores = [jnp.dot(qs[2 * hh + m], kt_scr[hh, m * dq:(m + 1) * dq, pl.ds(start, width)],
                          preferred_element_type=F32) for hh in heads for m in range(2)]
        for hh in heads:
            v1 = v1_scr[hh, pl.ds(start, width), :]
            for m in range(2):
                c = 2 * hh + m
                s = scores[c]
                if biased:
                    s = bias_ref[hh, :, 2 * tile - width:] + s
                m_prev = m_scr[c]
                m_next = jnp.maximum(m_prev, jnp.max(s, axis=1, keepdims=True))
                p = jnp.exp(s - jnp.tile(m_next, (1, width // LANES))).astype(BF16)
                alpha = jnp.exp(m_prev - m_next)
                m_scr[c] = m_next
                acc_scr[c] = jnp.tile(alpha, (1, 2)) * acc_scr[c] + jnp.dot(
                    p, v1, preferred_element_type=F32)

    def query_tile(i, carry):
        q_rows = pl.ds(pl.multiple_of(i * tile, tile), tile)
        m_scr[...] = jnp.full(m_scr.shape, NEG_INF, F32)
        acc_scr[...] = jnp.zeros(acc_scr.shape, F32)

        q = q_ref[q_rows, :] * (dq ** -0.5)
        qs = [q[:, c * dq:(c + 1) * dq] for c in range(2 * DIFF_HEADS_PER_STEP)]

        n_far = jnp.maximum(i - 1, 0)

        def far_body(j, inner):
            tile_update(qs, pl.multiple_of(j * (2 * tile), 2 * tile), 2 * tile, False)
            return inner

        lax.fori_loop(0, n_far // 2, far_body, 0)

        @pl.when(n_far % 2 == 1)
        def _():
            tile_update(qs, pl.multiple_of((n_far - 1) * tile, tile), tile, False)

        @pl.when(i > 0)
        def _():
            tile_update(qs, pl.multiple_of((i - 1) * tile, tile), 2 * tile, True)

        @pl.when(i == 0)
        def _():
            tile_update(qs, 0, tile, True)

        for hh in heads:
            c0, c1 = 2 * hh, 2 * hh + 1
            out = (acc_scr[c0, :, :dv] / acc_scr[c0, :, dv:]
                   - lam * (acc_scr[c1, :, :dv] / acc_scr[c1, :, dv:]))
            out = _rmsnorm_rows(out, gain_ref[...]) * (1.0 - lambda_init)
            o_ref[q_rows, hh * dv:(hh + 1) * dv] = out.astype(o_ref.dtype)
        return carry

    lax.fori_loop(0, seq // tile, query_tile, 0)


def _diff_attention(proj, bias_tiles, lam_params, head_gain, *, batch, seq, q_col, k_col, v_col,
                    tile, lambda_init):
    n = batch * seq
    hps = DIFF_HEADS_PER_STEP
    dv = DIFF_V_DIM
    gw = hps * dv
    qb, kb, vb = q_col // gw, k_col // gw, v_col // gw
    maps = 2 * hps
    return pl.pallas_call(
        functools.partial(_diff_kernel, tile=tile, lambda_init=lambda_init),
        grid=(batch, DIFF_HEADS // hps),
        in_specs=[
            pl.BlockSpec((seq, gw), lambda b, g: (b, qb + g)),
            pl.BlockSpec((seq, gw), lambda b, g: (b, kb + g)),
            pl.BlockSpec((seq, gw), lambda b, g: (b, vb + g)),
            pl.BlockSpec((hps, tile, 2 * tile), lambda b, g: (g, 0, 0)),
            pl.BlockSpec(lam_params.shape, lambda b, g: (0, 0)),
            pl.BlockSpec((1, dv), lambda b, g: (0, 0)),
        ],
        out_specs=pl.BlockSpec((seq, gw), lambda b, g: (b, g)),
        out_shape=jax.ShapeDtypeStruct((n, DIFF_HEADS * dv), BF16),
        scratch_shapes=[
            pltpu.VMEM((hps, dv, seq), BF16),
            pltpu.VMEM((hps, seq, dv + LANES), BF16),
            pltpu.VMEM((maps, tile, LANES), F32),
            pltpu.VMEM((maps, tile, dv + LANES), F32),
        ],
        compiler_params=_params("parallel", "parallel"),
        name="diff_attention",
    )(proj, proj, proj, bias_tiles, lam_params, head_gain)


def _ret_log_gamma(h):
    return math.log(1.0 - 2.0 ** (-5.0 - h))


def _retention_kernel(q_ref, k_ref, v_ref, g_ref, qcos_ref, qsin_ref, kcos_ref, ksin_ref,
                      o_ref, state_scr, decay_scr, *, ts):
    @pl.when(pl.program_id(1) == 0)
    def _():
        state_scr[...] = jnp.zeros(state_scr.shape, F32)

    c = RET_CHUNK
    d = RET_DIM
    row = lax.broadcasted_iota(jnp.int32, (c, c), 0).astype(F32)
    col = lax.broadcasted_iota(jnp.int32, (c, c), 1).astype(F32)
    rel = row - col
    heads = range(RET_HEADS)
    col_of = [slice(h * d, (h + 1) * d) for h in heads]
    log_gamma = [_ret_log_gamma(h) for h in heads]
    chunk_decay = [math.exp(c * lg) for lg in log_gamma]

    @pl.when(pl.program_id(1) == 0)
    def _():
        for h, lg in enumerate(log_gamma):
            decay_scr[0, h] = jnp.where(rel >= 0, jnp.exp(jnp.maximum(rel, 0.0) * lg), 0.0)
            decay_scr[1, h] = jnp.exp((c - 1 - row) * lg)
            decay_scr[2, h] = jnp.exp((row + 1) * lg)

    intra, key_decay, query_decay = (decay_scr.at[t] for t in range(3))
    nt_dims = (((1,), (1,)), ((), ()))
    tn_dims = (((0,), (0,)), ((), ()))

    states = [state_scr[h] for h in heads]
    for ci in range(ts // c):
        rows = slice(ci * c, (ci + 1) * c)
        qcos, qsin = qcos_ref[rows, :], qsin_ref[rows, :]
        kcos, ksin = kcos_ref[rows, :], ksin_ref[rows, :]
        qs, ks, kds, vs = [], [], [], []
        for h in heads:
            qh = q_ref[rows, col_of[h]].astype(F32)
            kh = k_ref[rows, col_of[h]].astype(F32)
            qh = qh * qcos + pltpu.roll(qh, d // 2, axis=1) * qsin
            kh = kh * kcos + pltpu.roll(kh, d // 2, axis=1) * ksin
            qs.append(qh.astype(BF16))
            ks.append(kh.astype(BF16))
            kds.append((kh * key_decay[h]).astype(BF16))
            vs.append(v_ref[rows, col_of[h]])
        scores = [lax.dot_general(qs[h], ks[h], nt_dims, preferred_element_type=F32)
                  for h in heads]
        cross = [jnp.dot(qs[h], states[h].astype(BF16), preferred_element_type=F32)
                 for h in heads]
        kv = [lax.dot_general(kds[h], vs[h], tn_dims, preferred_element_type=F32)
              for h in heads]
        inner = [jnp.dot((scores[h] * intra[h]).astype(BF16), vs[h], preferred_element_type=F32)
                 for h in heads]
        for h in heads:
            states[h] = chunk_decay[h] * states[h] + kv[h]
            out = _rmsnorm_rows(inner[h] + cross[h] * query_decay[h])
            gate = g_ref[rows, col_of[h]].astype(F32)
            o_ref[rows, col_of[h]] = (gate * jax.nn.sigmoid(gate) * out).astype(o_ref.dtype)
    for h in heads:
        state_scr[h] = states[h]


def _rotary_tables(seq):
    half = RET_DIM // 2
    inv_freq = 1.0 / (ROPE_BASE ** jnp.linspace(0.0, 1.0, half, dtype=F32))
    ang = jnp.arange(seq, dtype=F32)[:, None] * inv_freq[None, :]
    cos, sin = jnp.cos(ang), jnp.sin(ang)
    cos2 = jnp.concatenate([cos, cos], axis=-1)
    sin2 = jnp.concatenate([-sin, sin], axis=-1)
    kscale = RET_DIM ** -0.5
    return cos2, sin2, cos2 * kscale, sin2 * kscale


def _retention(proj, tables, *, batch, seq, mix, q_col, ts):
    n = batch * seq
    spb = seq // ts
    cb = q_col // mix
    act = lambda off: pl.BlockSpec((ts, mix), lambda b, s: (b * spb + s, cb + off))
    tab = pl.BlockSpec((ts, RET_DIM), lambda b, s: (s, 0))
    return pl.pallas_call(
        functools.partial(_retention_kernel, ts=ts),
        grid=(batch, spb),
        in_specs=[act(0), act(1), act(2), act(3), tab, tab, tab, tab],
        out_specs=pl.BlockSpec((ts, mix), lambda b, s: (b * spb + s, 0)),
        out_shape=jax.ShapeDtypeStruct((n, mix), BF16),
        scratch_shapes=[pltpu.VMEM((RET_HEADS, RET_DIM, RET_DIM), F32),
                        pltpu.VMEM((3, RET_HEADS, RET_CHUNK, RET_DIM), F32)],
        compiler_params=_params("parallel", "arbitrary"),
        name="retention",
    )(proj, proj, proj, proj, *tables)


def _merge_kernel(yp_ref, yd_ref, yr_ref, g0_ref, g1_ref, g2_ref, w_ref, o_ref):
    acc = None
    for n, (y_ref, g_ref) in enumerate(((yp_ref, g0_ref), (yd_ref, g1_ref), (yr_ref, g2_ref))):
        wide = jnp.dot(y_ref[...], w_ref[n], preferred_element_type=F32)
        term = jax.nn.sigmoid(g_ref[...].astype(F32)) * wide
        acc = term if acc is None else acc + term
    o_ref[...] = acc.astype(o_ref.dtype)


def _merge(y_pool, y_diff, y_ret, proj, w_branch, layer, *, gate_col, tm, tn):
    n, mix = y_pool.shape
    d = w_branch.shape[3]
    y_spec = pl.BlockSpec((tm, mix), lambda i, j: (i, 0))
    gate = lambda br: pl.BlockSpec((tm, tn), lambda i, j: (i, (gate_col + br * d) // tn + j))
    return pl.pallas_call(
        _merge_kernel,
        grid=(n // tm, d // tn),
        in_specs=[y_spec, y_spec, y_spec, gate(0), gate(1), gate(2),
                  pl.BlockSpec((None, N_BRANCHES, mix, tn), lambda i, j: (layer, 0, 0, j))],
        out_specs=pl.BlockSpec((tm, tn), lambda i, j: (i, j)),
        out_shape=jax.ShapeDtypeStruct((n, d), BF16),
        compiler_params=_params("parallel", "arbitrary"),
        name="merge",
    )(y_pool, y_diff, y_ret, proj, proj, proj, w_branch)


def _outproj_kernel(m_ref, w_ref, x_ref, g_ref, o_ref):
    mixed = jnp.dot(m_ref[...], w_ref[...], preferred_element_type=F32)
    o_ref[...] = x_ref[...] + _rmsnorm_rows(mixed, g_ref[...])


def _outproj(merged, w_out, x, gain, layer, *, tm):
    n, d = x.shape
    row = pl.BlockSpec((tm, d), lambda i: (i, 0))
    return pl.pallas_call(
        _outproj_kernel,
        grid=(n // tm,),
        in_specs=[row, pl.BlockSpec((None, d, d), lambda i: (layer, 0, 0)), row,
                  pl.BlockSpec((1, d), lambda i: (0, 0))],
        out_specs=row,
        out_shape=jax.ShapeDtypeStruct((n, d), F32),
        compiler_params=_params("parallel"),
        name="outproj",
    )(merged, w_out, x, gain)


def _mlp_kernel(x_ref, gpre_ref, wup_ref, wdown_ref, gpost_ref, o_ref, h_scr):
    acc_scr = o_ref
    k = pl.program_id(1)
    last = pl.num_programs(1) - 1
    chunks = _row_chunks(x_ref.shape[0])

    def ff_partial(rows):
        up = jnp.dot(h_scr[rows], wup_ref[...], preferred_element_type=F32)
        ff = jnp.square(jnp.maximum(up, 0.0)).astype(BF16)
        return jnp.dot(ff, wdown_ref[...], preferred_element_type=F32)

    @pl.when(k == 0)
    def _():
        for rows in chunks:
            h_scr[rows] = _rmsnorm_rows(x_ref[rows], gpre_ref[...]).astype(BF16)
            acc_scr[rows] = ff_partial(rows)

    @pl.when(jnp.logical_and(k > 0, k < last))
    def _():
        acc_scr[...] += ff_partial(slice(None))

    @pl.when(k == last)
    def _():
        for rows in chunks:
            ff = acc_scr[rows] + ff_partial(rows)
            o_ref[rows] = x_ref[rows] + _rmsnorm_rows(ff, gpost_ref[...])


def _mlp(x, gain_pre, w_up, w_down, gain_post, layer, *, tm, tf):
    n, d = x.shape
    f = w_up.shape[2]
    assert f // tf >= 2
    row = pl.BlockSpec((tm, d), lambda i, k: (i, 0))
    vec = pl.BlockSpec((1, d), lambda i, k: (0, 0))
    return pl.pallas_call(
        _mlp_kernel,
        grid=(n // tm, f // tf),
        in_specs=[row, vec, pl.BlockSpec((None, d, tf), lambda i, k: (layer, 0, k)),
                  pl.BlockSpec((None, tf, d), lambda i, k: (layer, k, 0)), vec],
        out_specs=row,
        out_shape=jax.ShapeDtypeStruct((n, d), F32),
        scratch_shapes=[pltpu.VMEM((tm, d), BF16)],
        compiler_params=_params("parallel", "arbitrary"),
        name="mlp",
    )(x, gain_pre, w_up, w_down, gain_post)


def _tiles(n_tokens, seq, in_cols, d_model, d_ff):
    pick = lambda total, want: want if total % want == 0 else total
    return dict(
        inproj=dict(tm=pick(n_tokens, 1024), tn=pick(in_cols, 2048)),
        pool=dict(ts=pick(seq, 512)),
        diff=dict(tile=pick(seq, 512)),
        ret=dict(ts=pick(seq, 512)),
        merge=dict(tm=pick(n_tokens, 512), tn=d_model),
        outproj=dict(tm=pick(n_tokens, 512)),
        mlp=dict(tm=pick(n_tokens, 512), tf=pick(d_ff, 2048)),
    )


def kernel(x, rel_bias, norm_pre_mix, norm_post_mix, norm_pre_mlp, norm_post_mlp, w_in, pool_w,
           pool_scale, diff_lambda, diff_head_norm, w_branch, w_out, w_up, w_down):
    batch, seq, d_model = x.shape
    depth, _, in_cols = w_in.shape
    mix = pool_scale.shape[1]
    d_ff = w_up.shape[2]
    n = batch * seq
    assert mix == DIFF_HEADS * DIFF_V_DIM == RET_HEADS * RET_DIM
    assert in_cols == 8 * mix + N_BRANCHES * d_model
    tiles = _tiles(n, seq, in_cols, d_model, d_ff)

    col = lambda idx: idx * mix
    bias_tiles = _bias_tiles(rel_bias, tile=tiles["diff"]["tile"])
    tables = _rotary_tables(seq)
    vec = lambda a: a.astype(F32).reshape(1, -1)
    w_in, pool_w, w_branch, w_out, w_up, w_down = (
        a.astype(BF16) for a in (w_in, pool_w, w_branch, w_out, w_up, w_down))

    xs = x.astype(F32).reshape(n, d_model)
    for l in range(depth):
        lambda_init = 0.8 - 0.6 * math.exp(-0.3 * l)
        proj = _inproj(xs, vec(norm_pre_mix[l]), w_in, l, **tiles["inproj"])
        y_pool = _pool(proj, pool_w, vec(pool_scale[l]), l,
                       batch=batch, seq=seq, mix=mix, **tiles["pool"])
        y_diff = _diff_attention(proj, bias_tiles, diff_lambda[l].astype(F32),
                                 vec(diff_head_norm[l]), batch=batch, seq=seq,
                                 q_col=col(1), k_col=col(2), v_col=col(3),
                                 lambda_init=lambda_init, **tiles["diff"])
        y_ret = _retention(proj, tables, batch=batch, seq=seq, mix=mix, q_col=col(4),
                           **tiles["ret"])
        merged = _merge(y_pool, y_diff, y_ret, proj, w_branch, l,
                        gate_col=col(8), **tiles["merge"])
        xs = _outproj(merged, w_out, xs, vec(norm_post_mix[l]), l, **tiles["outproj"])
        xs = _mlp(xs, vec(norm_pre_mlp[l]), w_up, w_down, vec(norm_post_mlp[l]), l,
                  **tiles["mlp"])
    return xs.reshape(batch, seq, d_model).astype(x.dtype)
```

```python
import functools
import math

import jax
import jax.numpy as jnp
from jax import lax
from jax.experimental import pallas as pl
from jax.experimental.pallas import tpu as pltpu

F32 = jnp.float32
BF16 = jnp.bfloat16

LANES = 128
POOL_WINDOWS = (2, 4, 8, 16)
POOL_HALO = 16
POOL_PAD = 8
DIFF_HEADS = 8
DIFF_QK_DIM = 64
DIFF_V_DIM = 128
DIFF_HEADS_PER_STEP = 2
RET_HEADS = 8
RET_DIM = 128
RET_CHUNK = 128
ROPE_BASE = 10000.0
N_BRANCHES = 3
REL_BUCKETS = 32
REL_MAX_DIST = 128
NORM_EPS = 1e-6
NORM_OVERLAP_ROWS = 256
NEG_INF = -1e30
VMEM_LIMIT = 56 * 1024 * 1024


def _params(*semantics):
    return pltpu.CompilerParams(dimension_semantics=semantics, vmem_limit_bytes=VMEM_LIMIT)


def _rmsnorm_rows(x, gain=None):
    y = x * lax.rsqrt(jnp.mean(x * x, axis=-1, keepdims=True) + NORM_EPS)
    return y if gain is None else y * gain


def _row_chunks(total, size=NORM_OVERLAP_ROWS):
    return [slice(r, r + size) for r in range(0, total, size)]


def _inproj_kernel(x_ref, g_ref, w_ref, o_ref, a_scr):
    j = pl.program_id(1)

    def project(rows):
        o_ref[rows] = jnp.dot(a_scr[rows], w_ref[...],
                              preferred_element_type=F32).astype(o_ref.dtype)

    @pl.when(j == 0)
    def _():
        for rows in _row_chunks(x_ref.shape[0]):
            a_scr[rows] = _rmsnorm_rows(x_ref[rows], g_ref[...]).astype(BF16)
            project(rows)

    @pl.when(j > 0)
    def _():
        project(slice(None))


def _inproj(x, gain, w, layer, *, tm, tn):
    n, d = x.shape
    c = w.shape[2]
    return pl.pallas_call(
        _inproj_kernel,
        grid=(n // tm, c // tn),
        in_specs=[
            pl.BlockSpec((tm, d), lambda i, j: (i, 0)),
            pl.BlockSpec((1, d), lambda i, j: (0, 0)),
            pl.BlockSpec((None, d, tn), lambda i, j: (layer, 0, j)),
        ],
        out_specs=pl.BlockSpec((tm, tn), lambda i, j: (i, j)),
        out_shape=jax.ShapeDtypeStruct((n, c), BF16),
        scratch_shapes=[pltpu.VMEM((tm, d), BF16)],
        compiler_params=_params("parallel", "arbitrary"),
        name="inproj",
    )(x, gain, w)


def _pool_kernel(u_ref, halo_ref, w_ref, scale_ref, o_ref, ext_scr, stage_scr, *, ts):
    s = pl.program_id(1)
    mix = u_ref.shape[1]
    gdim = mix // len(POOL_WINDOWS)
    lo, first = POOL_PAD, POOL_PAD + POOL_HALO
    total = first + ts
    cur = u_ref[...].astype(F32)
    ext_scr[0:lo, :] = jnp.zeros((lo, mix), F32)
    ext_scr[lo:first, :] = jnp.where(s > 0, halo_ref[...].astype(F32), 0.0)
    ext_scr[first:, :] = cur
    stage_scr[:, 0:lo, :] = jnp.zeros((2, lo, gdim), F32)
    t = s * ts + lax.broadcasted_iota(jnp.int32, (ts, 1), 0)
    for g, win in enumerate(POOL_WINDOWS):
        cols = slice(g * gdim, (g + 1) * gdim)
        src = ext_scr.at[:, cols]
        for stage, shift in enumerate(1 << k for k in range(win.bit_length() - 1)):
            val = src[lo:total, :] + src[lo - shift:total - shift, :]
            if 2 * shift < win:
                src = stage_scr.at[stage % 2]
                src[lo:total, :] = val
        tot = val[first - lo:, :]
        count = jnp.minimum(t + 1, win).astype(F32)
        delta = tot / count - cur[:, cols]
        y = jnp.dot(delta.astype(BF16), w_ref[g], preferred_element_type=F32)
        o_ref[:, cols] = (y * scale_ref[:, cols]).astype(o_ref.dtype)


def _pool(proj, pool_w, pool_scale, layer, *, batch, seq, mix, ts):
    n = batch * seq
    spb = seq // ts
    hpt = ts // POOL_HALO
    g = len(POOL_WINDOWS)
    return pl.pallas_call(
        functools.partial(_pool_kernel, ts=ts),
        grid=(batch, spb),
        in_specs=[
            pl.BlockSpec((ts, mix), lambda b, s: (b * spb + s, 0)),
            pl.BlockSpec((POOL_HALO, mix),
                         lambda b, s: (jnp.maximum((b * spb + s) * hpt - 1, 0), 0)),
            pl.BlockSpec((None, g, mix // g, mix // g), lambda b, s: (layer, 0, 0, 0)),
            pl.BlockSpec((1, mix), lambda b, s: (0, 0)),
        ],
        out_specs=pl.BlockSpec((ts, mix), lambda b, s: (b * spb + s, 0)),
        out_shape=jax.ShapeDtypeStruct((n, mix), BF16),
        scratch_shapes=[pltpu.VMEM((POOL_PAD + POOL_HALO + ts, mix), F32),
                        pltpu.VMEM((2, POOL_PAD + POOL_HALO + ts, mix // g), F32)],
        compiler_params=_params("parallel", "arbitrary"),
        name="pool",
    )(proj, proj, pool_w, pool_scale)


def _bias_tiles_kernel(tbl_ref, o_ref, *, tile):
    h = pl.program_id(0)
    exact = REL_BUCKETS // 2
    blk = REL_MAX_DIST
    row = lax.broadcasted_iota(jnp.int32, (blk, blk), 0)
    col = lax.broadcasted_iota(jnp.int32, (blk, blk), 1)
    far = tbl_ref[h * REL_BUCKETS + REL_BUCKETS - 1]
    for which in range(2):
        for rb in range(tile // blk):
            for cb in range(tile // blk):
                base = which * tile + (rb - cb) * blk
                out = o_ref.at[0, rb * blk:(rb + 1) * blk,
                               (1 - which) * tile + cb * blk:(1 - which) * tile + (cb + 1) * blk]
                if base + (blk - 1) < 0:
                    out[...] = jnp.full((blk, blk), NEG_INF, F32)
                    continue
                if base - (blk - 1) >= REL_MAX_DIST:
                    out[...] = jnp.zeros((blk, blk), F32)
                    continue
                dist = row - col + base
                n = jnp.maximum(dist, 0)
                nf = jnp.maximum(n, 1).astype(F32)
                large = exact + (jnp.log(nf / exact) / math.log(REL_MAX_DIST / exact)
                                 * (REL_BUCKETS - exact)).astype(jnp.int32)
                large = jnp.minimum(large, REL_BUCKETS - 1)
                bucket = jnp.where(n < exact, n, large)
                bias = jnp.zeros((blk, blk), F32)
                for b in range(REL_BUCKETS):
                    bias = jnp.where(bucket == b, tbl_ref[h * REL_BUCKETS + b], bias)
                out[...] = jnp.where(dist >= 0, bias - far, NEG_INF)


def _bias_tiles(rel_bias, *, tile):
    heads = rel_bias.shape[1]
    tbl = rel_bias.astype(F32).T.reshape(-1)
    return pl.pallas_call(
        functools.partial(_bias_tiles_kernel, tile=tile),
        grid=(heads,),
        in_specs=[pl.BlockSpec(memory_space=pltpu.SMEM)],
        out_specs=pl.BlockSpec((1, tile, 2 * tile), lambda h: (h, 0, 0)),
        out_shape=jax.ShapeDtypeStruct((heads, tile, 2 * tile), F32),
        compiler_params=_params("arbitrary"),
        name="bias_tiles",
    )(tbl)


def _diff_kernel(q_ref, k_ref, v_ref, bias_ref, lam_ref, gain_ref, o_ref,
                 kt_scr, v1_scr, m_scr, acc_scr, *, tile, lambda_init):
    seq = k_ref.shape[0]
    dq = DIFF_QK_DIM
    dv = DIFF_V_DIM
    heads = range(DIFF_HEADS_PER_STEP)

    for hh in heads:
        for c in range(seq // tile):
            rows = slice(c * tile, (c + 1) * tile)
            kt_scr[hh, :, rows] = k_ref[rows, hh * dv:(hh + 1) * dv].astype(F32).T.astype(BF16)
        v1_scr[hh, :, :dv] = v_ref[:, hh * dv:(hh + 1) * dv]
        v1_scr[hh, :, dv:] = jnp.ones((seq, LANES), BF16)

    lp = lam_ref[...]
    lam = (jnp.exp(jnp.sum(lp[0:1] * lp[1:2], axis=-1, keepdims=True))
           - jnp.exp(jnp.sum(lp[2:3] * lp[3:4], axis=-1, keepdims=True)) + lambda_init)

    def tile_update(qs, start, width, first):
        scores = [jnp.dot(qs[2 * hh + m], kt_scr[hh, m * dq:(m + 1) * dq, pl.ds(start, width)],
                          preferred_element_type=F32) for hh in heads for m in range(2)]
        for hh in heads:
            v1 = v1_scr[hh, pl.ds(start, width), :]
            for m in range(2):
                c = 2 * hh + m
                s = scores[c]
                if first:
                    s = bias_ref[hh, :, 2 * tile - width:] + s
                    m_next = jnp.broadcast_to(jnp.max(s, axis=1, keepdims=True), (tile, LANES))
                else:
                    m_prev = m_scr[c]
                    m_next = jnp.maximum(m_prev, jnp.max(s, axis=1, keepdims=True))
                p = jnp.exp(s - jnp.tile(m_next, (1, width // LANES))).astype(BF16)
                m_scr[c] = m_next
                pv = jnp.dot(p, v1, preferred_element_type=F32)
                if first:
                    acc_scr[c] = pv
                else:
                    alpha = jnp.exp(m_prev - m_next)
                    acc_scr[c] = jnp.tile(alpha, (1, 2)) * acc_scr[c] + pv

    def query_tile(i, carry):
        q_rows = pl.ds(pl.multiple_of(i * tile, tile), tile)
        q = q_ref[q_rows, :] * (dq ** -0.5)
        qs = [q[:, c * dq:(c + 1) * dq] for c in range(2 * DIFF_HEADS_PER_STEP)]

        @pl.when(i > 0)
        def _():
            tile_update(qs, pl.multiple_of((i - 1) * tile, tile), 2 * tile, True)

        @pl.when(i == 0)
        def _():
            tile_update(qs, 0, tile, True)

        n_far = jnp.maximum(i - 1, 0)

        def far_body(j, inner):
            tile_update(qs, pl.multiple_of(j * (2 * tile), 2 * tile), 2 * tile, False)
            return inner

        lax.fori_loop(0, n_far // 2, far_body, 0)

        @pl.when(n_far % 2 == 1)
        def _():
            tile_update(qs, pl.multiple_of((n_far - 1) * tile, tile), tile, False)

        for hh in heads:
            c0, c1 = 2 * hh, 2 * hh + 1
            out = (acc_scr[c0, :, :dv] / acc_scr[c0, :, dv:]
                   - lam * (acc_scr[c1, :, :dv] / acc_scr[c1, :, dv:]))
            out = _rmsnorm_rows(out, gain_ref[...]) * (1.0 - lambda_init)
            o_ref[q_rows, hh * dv:(hh + 1) * dv] = out.astype(o_ref.dtype)
        return carry

    lax.fori_loop(0, seq // tile, query_tile, 0)


def _diff_attention(proj, bias_tiles, lam_params, head_gain, *, batch, seq, q_col, k_col, v_col,
                    tile, lambda_init):
    n = batch * seq
    hps = DIFF_HEADS_PER_STEP
    dv = DIFF_V_DIM
    gw = hps * dv
    qb, kb, vb = q_col // gw, k_col // gw, v_col // gw
    maps = 2 * hps
    return pl.pallas_call(
        functools.partial(_diff_kernel, tile=tile, lambda_init=lambda_init),
        grid=(batch, DIFF_HEADS // hps),
        in_specs=[
            pl.BlockSpec((seq, gw), lambda b, g: (b, qb + g)),
            pl.BlockSpec((seq, gw), lambda b, g: (b, kb + g)),
            pl.BlockSpec((seq, gw), lambda b, g: (b, vb + g)),
            pl.BlockSpec((hps, tile, 2 * tile), lambda b, g: (g, 0, 0)),
            pl.BlockSpec(lam_params.shape, lambda b, g: (0, 0)),
            pl.BlockSpec((1, dv), lambda b, g: (0, 0)),
        ],
        out_specs=pl.BlockSpec((seq, gw), lambda b, g: (b, g)),
        out_shape=jax.ShapeDtypeStruct((n, DIFF_HEADS * dv), BF16),
        scratch_shapes=[
            pltpu.VMEM((hps, dv, seq), BF16),
            pltpu.VMEM((hps, seq, dv + LANES), BF16),
            pltpu.VMEM((maps, tile, LANES), F32),
            pltpu.VMEM((maps, tile, dv + LANES), F32),
        ],
        compiler_params=_params("parallel", "parallel"),
        name="diff_attention",
    )(proj, proj, proj, bias_tiles, lam_params, head_gain)


def _ret_log_gamma(h):
    return math.log(1.0 - 2.0 ** (-5.0 - h))


def _retention_kernel(q_ref, k_ref, v_ref, g_ref, qcos_ref, qsin_ref, kcos_ref, ksin_ref,
                      o_ref, state_scr, decay_scr, *, ts):
    @pl.when(pl.program_id(1) == 0)
    def _():
        state_scr[...] = jnp.zeros(state_scr.shape, F32)

    c = RET_CHUNK
    d = RET_DIM
    row = lax.broadcasted_iota(jnp.int32, (c, c), 0).astype(F32)
    col = lax.broadcasted_iota(jnp.int32, (c, c), 1).astype(F32)
    rel = row - col
    heads = range(RET_HEADS)
    col_of = [slice(h * d, (h + 1) * d) for h in heads]
    log_gamma = [_ret_log_gamma(h) for h in heads]
    chunk_decay = [math.exp(c * lg) for lg in log_gamma]

    @pl.when(pl.program_id(1) == 0)
    def _():
        for h, lg in enumerate(log_gamma):
            decay_scr[0, h] = jnp.where(rel >= 0, jnp.exp(jnp.maximum(rel, 0.0) * lg), 0.0)
            decay_scr[1, h] = jnp.exp((c - 1 - row) * lg)
            decay_scr[2, h] = jnp.exp((row + 1) * lg)

    intra, key_decay, query_decay = (decay_scr.at[t] for t in range(3))
    nt_dims = (((1,), (1,)), ((), ()))
    tn_dims = (((0,), (0,)), ((), ()))

    states = [state_scr[h] for h in heads]
    for ci in range(ts // c):
        rows = slice(ci * c, (ci + 1) * c)
        qcos, qsin = qcos_ref[rows, :], qsin_ref[rows, :]
        kcos, ksin = kcos_ref[rows, :], ksin_ref[rows, :]
        qs, ks, kds, vs = [], [], [], []
        for h in heads:
            qh = q_ref[rows, col_of[h]].astype(F32)
            kh = k_ref[rows, col_of[h]].astype(F32)
            qh = qh * qcos + pltpu.roll(qh, d // 2, axis=1) * qsin
            kh = kh * kcos + pltpu.roll(kh, d // 2, axis=1) * ksin
            qs.append(qh.astype(BF16))
            ks.append(kh.astype(BF16))
            kds.append((kh * key_decay[h]).astype(BF16))
            vs.append(v_ref[rows, col_of[h]])
        scores = [lax.dot_general(qs[h], ks[h], nt_dims, preferred_element_type=F32)
                  for h in heads]
        cross = [jnp.dot(qs[h], states[h].astype(BF16), preferred_element_type=F32)
                 for h in heads]
        kv = [lax.dot_general(kds[h], vs[h], tn_dims, preferred_element_type=F32)
              for h in heads]
        inner = [jnp.dot((scores[h] * intra[h]).astype(BF16), vs[h], preferred_element_type=F32)
                 for h in heads]
        for h in heads:
            states[h] = chunk_decay[h] * states[h] + kv[h]
            out = _rmsnorm_rows(inner[h] + cross[h] * query_decay[h])
            gate = g_ref[rows, col_of[h]].astype(F32)
            o_ref[rows, col_of[h]] = (gate * jax.nn.sigmoid(gate) * out).astype(o_ref.dtype)
    for h in heads:
        state_scr[h] = states[h]


def _rotary_tables(seq):
    half = RET_DIM // 2
    inv_freq = 1.0 / (ROPE_BASE ** jnp.linspace(0.0, 1.0, half, dtype=F32))
    ang = jnp.arange(seq, dtype=F32)[:, None] * inv_freq[None, :]
    cos, sin = jnp.cos(ang), jnp.sin(ang)
    cos2 = jnp.concatenate([cos, cos], axis=-1)
    sin2 = jnp.concatenate([-sin, sin], axis=-1)
    kscale = RET_DIM ** -0.5
    return cos2, sin2, cos2 * kscale, sin2 * kscale


def _retention(proj, tables, *, batch, seq, mix, q_col, ts):
    n = batch * seq
    spb = seq // ts
    cb = q_col // mix
    act = lambda off: pl.BlockSpec((ts, mix), lambda b, s: (b * spb + s, cb + off))
    tab = pl.BlockSpec((ts, RET_DIM), lambda b, s: (s, 0))
    return pl.pallas_call(
        functools.partial(_retention_kernel, ts=ts),
        grid=(batch, spb),
        in_specs=[act(0), act(1), act(2), act(3), tab, tab, tab, tab],
        out_specs=pl.BlockSpec((ts, mix), lambda b, s: (b * spb + s, 0)),
        out_shape=jax.ShapeDtypeStruct((n, mix), BF16),
        scratch_shapes=[pltpu.VMEM((RET_HEADS, RET_DIM, RET_DIM), F32),
                        pltpu.VMEM((3, RET_HEADS, RET_CHUNK, RET_DIM), F32)],
        compiler_params=_params("parallel", "arbitrary"),
        name="retention",
    )(proj, proj, proj, proj, *tables)


def _merge_kernel(yp_ref, yd_ref, yr_ref, g0_ref, g1_ref, g2_ref, w_ref, o_ref):
    acc = None
    for n, (y_ref, g_ref) in enumerate(((yp_ref, g0_ref), (yd_ref, g1_ref), (yr_ref, g2_ref))):
        wide = jnp.dot(y_ref[...], w_ref[n], preferred_element_type=F32)
        term = jax.nn.sigmoid(g_ref[...].astype(F32)) * wide
        acc = term if acc is None else acc + term
    o_ref[...] = acc.astype(o_ref.dtype)


def _merge(y_pool, y_diff, y_ret, proj, w_branch, layer, *, gate_col, tm, tn):
    n, mix = y_pool.shape
    d = w_branch.shape[3]
    y_spec = pl.BlockSpec((tm, mix), lambda i, j: (i, 0))
    gate = lambda br: pl.BlockSpec((tm, tn), lambda i, j: (i, (gate_col + br * d) // tn + j))
    return pl.pallas_call(
        _merge_kernel,
        grid=(n // tm, d // tn),
        in_specs=[y_spec, y_spec, y_spec, gate(0), gate(1), gate(2),
                  pl.BlockSpec((None, N_BRANCHES, mix, tn), lambda i, j: (layer, 0, 0, j))],
        out_specs=pl.BlockSpec((tm, tn), lambda i, j: (i, j)),
        out_shape=jax.ShapeDtypeStruct((n, d), BF16),
        compiler_params=_params("parallel", "arbitrary"),
        name="merge",
    )(y_pool, y_diff, y_ret, proj, proj, proj, w_branch)


def _outproj_kernel(m_ref, w_ref, x_ref, g_ref, o_ref):
    mixed = jnp.dot(m_ref[...], w_ref[...], preferred_element_type=F32)
    o_ref[...] = x_ref[...] + _rmsnorm_rows(mixed, g_ref[...])


def _outproj(merged, w_out, x, gain, layer, *, tm):
    n, d = x.shape
    row = pl.BlockSpec((tm, d), lambda i: (i, 0))
    return pl.pallas_call(
        _outproj_kernel,
        grid=(n // tm,),
        in_specs=[row, pl.BlockSpec((None, d, d), lambda i: (layer, 0, 0)), row,
                  pl.BlockSpec((1, d), lambda i: (0, 0))],
        out_specs=row,
        out_shape=jax.ShapeDtypeStruct((n, d), F32),
        compiler_params=_params("parallel"),
        name="outproj",
    )(merged, w_out, x, gain)


def _mlp_kernel(x_ref, gpre_ref, wup_ref, wdown_ref, gpost_ref, o_ref, h_scr):
    acc_scr = o_ref
    k = pl.program_id(1)
    last = pl.num_programs(1) - 1
    chunks = _row_chunks(x_ref.shape[0])

    def ff_partial(rows):
        up = jnp.dot(h_scr[rows], wup_ref[...], preferred_element_type=F32)
        ff = jnp.square(jnp.maximum(up, 0.0)).astype(BF16)
        return jnp.dot(ff, wdown_ref[...], preferred_element_type=F32)

    @pl.when(k == 0)
    def _():
        for rows in chunks:
            h_scr[rows] = _rmsnorm_rows(x_ref[rows], gpre_ref[...]).astype(BF16)
            acc_scr[rows] = ff_partial(rows)

    @pl.when(jnp.logical_and(k > 0, k < last))
    def _():
        acc_scr[...] += ff_partial(slice(None))

    @pl.when(k == last)
    def _():
        for rows in chunks:
            ff = acc_scr[rows] + ff_partial(rows)
            o_ref[rows] = x_ref[rows] + _rmsnorm_rows(ff, gpost_ref[...])


def _mlp(x, gain_pre, w_up, w_down, gain_post, layer, *, tm, tf):
    n, d = x.shape
    f = w_up.shape[2]
    assert f // tf >= 2
    row = pl.BlockSpec((tm, d), lambda i, k: (i, 0))
    vec = pl.BlockSpec((1, d), lambda i, k: (0, 0))
    return pl.pallas_call(
        _mlp_kernel,
        grid=(n // tm, f // tf),
        in_specs=[row, vec, pl.BlockSpec((None, d, tf), lambda i, k: (layer, 0, k)),
                  pl.BlockSpec((None, tf, d), lambda i, k: (layer, k, 0)), vec],
        out_specs=row,
        out_shape=jax.ShapeDtypeStruct((n, d), F32),
        scratch_shapes=[pltpu.VMEM((tm, d), BF16)],
        compiler_params=_params("parallel", "arbitrary"),
        name="mlp",
    )(x, gain_pre, w_up, w_down, gain_post)


def _tiles(n_tokens, seq, in_cols, d_model, d_ff):
    pick = lambda total, want: want if total % want == 0 else total
    return dict(
        inproj=dict(tm=pick(n_tokens, 1024), tn=pick(in_cols, 2048)),
        pool=dict(ts=pick(seq, 512)),
        diff=dict(tile=pick(seq, 512)),
        ret=dict(ts=pick(seq, 512)),
        merge=dict(tm=pick(n_tokens, 512), tn=d_model),
        outproj=dict(tm=pick(n_tokens, 512)),
        mlp=dict(tm=pick(n_tokens, 512), tf=pick(d_ff, 2048)),
    )


def kernel(x, rel_bias, norm_pre_mix, norm_post_mix, norm_pre_mlp, norm_post_mlp, w_in, pool_w,
           pool_scale, diff_lambda, diff_head_norm, w_branch, w_out, w_up, w_down):
    batch, seq, d_model = x.shape
    depth, _, in_cols = w_in.shape
    mix = pool_scale.shape[1]
    d_ff = w_up.shape[2]
    n = batch * seq
    assert mix == DIFF_HEADS * DIFF_V_DIM == RET_HEADS * RET_DIM
    assert in_cols == 8 * mix + N_BRANCHES * d_model
    tiles = _tiles(n, seq, in_cols, d_model, d_ff)

    col = lambda idx: idx * mix
    bias_tiles = _bias_tiles(rel_bias, tile=tiles["diff"]["tile"])
    tables = _rotary_tables(seq)
    vec = lambda a: a.astype(F32).reshape(1, -1)
    w_in, pool_w, w_branch, w_out, w_up, w_down = (
        a.astype(BF16) for a in (w_in, pool_w, w_branch, w_out, w_up, w_down))

    xs = x.astype(F32).reshape(n, d_model)
    for l in range(depth):
        lambda_init = 0.8 - 0.6 * math.exp(-0.3 * l)
        proj = _inproj(xs, vec(norm_pre_mix[l]), w_in, l, **tiles["inproj"])
        y_pool = _pool(proj, pool_w, vec(pool_scale[l]), l,
                       batch=batch, seq=seq, mix=mix, **tiles["pool"])
        y_diff = _diff_attention(proj, bias_tiles, diff_lambda[l].astype(F32),
                                 vec(diff_head_norm[l]), batch=batch, seq=seq,
                                 q_col=col(1), k_col=col(2), v_col=col(3),
                                 lambda_init=lambda_init, **tiles["diff"])
        y_ret = _retention(proj, tables, batch=batch, seq=seq, mix=mix, q_col=col(4),
                           **tiles["ret"])
        merged = _merge(y_pool, y_diff, y_ret, proj, w_branch, l,
                        gate_col=col(8), **tiles["merge"])
        xs = _outproj(merged, w_out, xs, vec(norm_post_mix[l]), l, **tiles["outproj"])
        xs = _mlp(xs, vec(norm_pre_mlp[l]), w_up, w_down, vec(norm_post_mlp[l]), l,
                  **tiles["mlp"])
    return xs.reshape(batch, seq, d_model).astype(x.dtype)
```

```python
import functools
import math

import jax
import jax.numpy as jnp
from jax import lax
from jax.experimental import pallas as pl
from jax.experimental.pallas import tpu as pltpu

F32 = jnp.float32
BF16 = jnp.bfloat16

LANES = 128
POOL_WINDOWS = (2, 4, 8, 16)
POOL_HALO = 16
POOL_PAD = 8
DIFF_HEADS = 8
DIFF_QK_DIM = 64
DIFF_V_DIM = 128
DIFF_HEADS_PER_STEP = 2
RET_HEADS = 8
RET_DIM = 128
RET_CHUNK = 128
ROPE_BASE = 10000.0
N_BRANCHES = 3
REL_BUCKETS = 32
REL_MAX_DIST = 128
NORM_EPS = 1e-6
NORM_OVERLAP_ROWS = 256
NEG_INF = -1e30
VMEM_LIMIT = 56 * 1024 * 1024


def _params(*semantics):
    return pltpu.CompilerParams(dimension_semantics=semantics, vmem_limit_bytes=VMEM_LIMIT)


def _rmsnorm_rows(x, gain=None):
    y = x * lax.rsqrt(jnp.mean(x * x, axis=-1, keepdims=True) + NORM_EPS)
    return y if gain is None else y * gain


def _row_chunks(total, size=NORM_OVERLAP_ROWS):
    return [slice(r, r + size) for r in range(0, total, size)]


def _inproj_kernel(x_ref, g_ref, w_ref, o_ref, a_scr):
    j = pl.program_id(1)

    def project(rows):
        o_ref[rows] = jnp.dot(a_scr[rows], w_ref[...],
                              preferred_element_type=F32).astype(o_ref.dtype)

    @pl.when(j == 0)
    def _():
        for rows in _row_chunks(x_ref.shape[0]):
            a_scr[rows] = _rmsnorm_rows(x_ref[rows], g_ref[...]).astype(BF16)
            project(rows)

    @pl.when(j > 0)
    def _():
        project(slice(None))


def _inproj(x, gain, w, layer, *, tm, tn):
    n, d = x.shape
    c = w.shape[2]
    return pl.pallas_call(
        _inproj_kernel,
        grid=(n // tm, c // tn),
        in_specs=[
            pl.BlockSpec((tm, d), lambda i, j: (i, 0)),
            pl.BlockSpec((1, d), lambda i, j: (0, 0)),
            pl.BlockSpec((None, d, tn), lambda i, j: (layer, 0, j)),
        ],
        out_specs=pl.BlockSpec((tm, tn), lambda i, j: (i, j)),
        out_shape=jax.ShapeDtypeStruct((n, c), BF16),
        scratch_shapes=[pltpu.VMEM((tm, d), BF16)],
        compiler_params=_params("parallel", "arbitrary"),
        name="inproj",
    )(x, gain, w)


def _pool_kernel(u_ref, halo_ref, w_ref, scale_ref, o_ref, ext_scr, stage_scr, *, ts):
    s = pl.program_id(1)
    mix = u_ref.shape[1]
    gdim = mix // len(POOL_WINDOWS)
    lo, first = POOL_PAD, POOL_PAD + POOL_HALO
    total = first + ts
    cur = u_ref[...].astype(F32)
    ext_scr[0:lo, :] = jnp.zeros((lo, mix), F32)
    ext_scr[lo:first, :] = jnp.where(s > 0, halo_ref[...].astype(F32), 0.0)
    ext_scr[first:, :] = cur
    stage_scr[:, 0:lo, :] = jnp.zeros((2, lo, gdim), F32)
    t = s * ts + lax.broadcasted_iota(jnp.int32, (ts, 1), 0)
    for g, win in enumerate(POOL_WINDOWS):
        cols = slice(g * gdim, (g + 1) * gdim)
        src = ext_scr.at[:, cols]
        for stage, shift in enumerate(1 << k for k in range(win.bit_length() - 1)):
            val = src[lo:total, :] + src[lo - shift:total - shift, :]
            if 2 * shift < win:
                src = stage_scr.at[stage % 2]
                src[lo:total, :] = val
        tot = val[first - lo:, :]
        count = jnp.minimum(t + 1, win).astype(F32)
        delta = tot / count - cur[:, cols]
        y = jnp.dot(delta.astype(BF16), w_ref[g], preferred_element_type=F32)
        o_ref[:, cols] = (y * scale_ref[:, cols]).astype(o_ref.dtype)


def _pool(proj, pool_w, pool_scale, layer, *, batch, seq, mix, ts):
    n = batch * seq
    spb = seq // ts
    hpt = ts // POOL_HALO
    g = len(POOL_WINDOWS)
    return pl.pallas_call(
        functools.partial(_pool_kernel, ts=ts),
        grid=(batch, spb),
        in_specs=[
            pl.BlockSpec((ts, mix), lambda b, s: (b * spb + s, 0)),
            pl.BlockSpec((POOL_HALO, mix),
                         lambda b, s: (jnp.maximum((b * spb + s) * hpt - 1, 0), 0)),
            pl.BlockSpec((None, g, mix // g, mix // g), lambda b, s: (layer, 0, 0, 0)),
            pl.BlockSpec((1, mix), lambda b, s: (0, 0)),
        ],
        out_specs=pl.BlockSpec((ts, mix), lambda b, s: (b * spb + s, 0)),
        out_shape=jax.ShapeDtypeStruct((n, mix), BF16),
        scratch_shapes=[pltpu.VMEM((POOL_PAD + POOL_HALO + ts, mix), F32),
                        pltpu.VMEM((2, POOL_PAD + POOL_HALO + ts, mix // g), F32)],
        compiler_params=_params("parallel", "arbitrary"),
        name="pool",
    )(proj, proj, pool_w, pool_scale)


def _bias_tiles_kernel(tbl_ref, o_ref, *, tile):
    h = pl.program_id(0)
    exact = REL_BUCKETS // 2
    blk = REL_MAX_DIST
    row = lax.broadcasted_iota(jnp.int32, (blk, blk), 0)
    col = lax.broadcasted_iota(jnp.int32, (blk, blk), 1)
    far = tbl_ref[h * REL_BUCKETS + REL_BUCKETS - 1]
    for which in range(2):
        for rb in range(tile // blk):
            for cb in range(tile // blk):
                base = which * tile + (rb - cb) * blk
                out = o_ref.at[0, rb * blk:(rb + 1) * blk,
                               (1 - which) * tile + cb * blk:(1 - which) * tile + (cb + 1) * blk]
                if base + (blk - 1) < 0:
                    out[...] = jnp.full((blk, blk), NEG_INF, F32)
                    continue
                if base - (blk - 1) >= REL_MAX_DIST:
                    out[...] = jnp.zeros((blk, blk), F32)
                    continue
                dist = row - col + base
                n = jnp.maximum(dist, 0)
                nf = jnp.maximum(n, 1).astype(F32)
                large = exact + (jnp.log(nf / exact) / math.log(REL_MAX_DIST / exact)
                                 * (REL_BUCKETS - exact)).astype(jnp.int32)
                large = jnp.minimum(large, REL_BUCKETS - 1)
                bucket = jnp.where(n < exact, n, large)
                bias = jnp.zeros((blk, blk), F32)
                for b in range(REL_BUCKETS):
                    bias = jnp.where(bucket == b, tbl_ref[h * REL_BUCKETS + b], bias)
                out[...] = jnp.where(dist >= 0, bias - far, NEG_INF)


def _bias_tiles(rel_bias, *, tile):
    heads = rel_bias.shape[1]
    tbl = rel_bias.astype(F32).T.reshape(-1)
    return pl.pallas_call(
        functools.partial(_bias_tiles_kernel, tile=tile),
        grid=(heads,),
        in_specs=[pl.BlockSpec(memory_space=pltpu.SMEM)],
        out_specs=pl.BlockSpec((1, tile, 2 * tile), lambda h: (h, 0, 0)),
        out_shape=jax.ShapeDtypeStruct((heads, tile, 2 * tile), F32),
        compiler_params=_params("arbitrary"),
        name="bias_tiles",
    )(tbl)


def _diff_kernel(q_ref, k_ref, v_ref, bias_ref, lam_ref, gain_ref, o_ref,
                 kt_scr, v1_scr, m_scr, acc_scr, *, tile, lambda_init):
    seq = k_ref.shape[0]
    dq = DIFF_QK_DIM
    dv = DIFF_V_DIM
    heads = range(DIFF_HEADS_PER_STEP)

    for hh in heads:
        for c in range(seq // tile):
            rows = slice(c * tile, (c + 1) * tile)
            kt_scr[hh, :, rows] = k_ref[rows, hh * dv:(hh + 1) * dv].astype(F32).T.astype(BF16)
        v1_scr[hh, :, :dv] = v_ref[:, hh * dv:(hh + 1) * dv]
        v1_scr[hh, :, dv:] = jnp.ones((seq, LANES), BF16)

    lp = lam_ref[...]
    lam = (jnp.exp(jnp.sum(lp[0:1] * lp[1:2], axis=-1, keepdims=True))
           - jnp.exp(jnp.sum(lp[2:3] * lp[3:4], axis=-1, keepdims=True)) + lambda_init)

    def add_bias(hh, s):
        blk = REL_MAX_DIST
        if s.shape[1] == tile:
            return bias_ref[hh, :, tile:] + s
        corner = bias_ref[hh, :blk, tile - blk:tile] + s[:blk, tile - blk:tile]
        top = jnp.concatenate([s[:blk, :tile - blk], corner], axis=1)
        prev = jnp.concatenate([top, s[blk:, :tile]], axis=0)
        return jnp.concatenate([prev, bias_ref[hh, :, tile:] + s[:, tile:]], axis=1)

    def tile_update(qs, start, width, first):
        scores = [jnp.dot(qs[2 * hh + m], kt_scr[hh, m * dq:(m + 1) * dq, pl.ds(start, width)],
                          preferred_element_type=F32) for hh in heads for m in range(2)]
        for hh in heads:
            v1 = v1_scr[hh, pl.ds(start, width), :]
            for m in range(2):
                c = 2 * hh + m
                s = scores[c]
                if first:
                    s = add_bias(hh, s)
                    m_next = jnp.broadcast_to(jnp.max(s, axis=1, keepdims=True), (tile, LANES))
                else:
                    m_prev = m_scr[c]
                    m_next = jnp.maximum(m_prev, jnp.max(s, axis=1, keepdims=True))
                p = jnp.exp(s - jnp.tile(m_next, (1, width // LANES))).astype(BF16)
                m_scr[c] = m_next
                pv = jnp.dot(p, v1, preferred_element_type=F32)
                if first:
                    acc_scr[c] = pv
                else:
                    alpha = jnp.exp(m_prev - m_next)
                    acc_scr[c] = jnp.tile(alpha, (1, 2)) * acc_scr[c] + pv

    def query_tile(i, carry):
        q_rows = pl.ds(pl.multiple_of(i * tile, tile), tile)
        q = q_ref[q_rows, :] * (dq ** -0.5)
        qs = [q[:, c * dq:(c + 1) * dq] for c in range(2 * DIFF_HEADS_PER_STEP)]

        @pl.when(i > 0)
        def _():
            tile_update(qs, pl.multiple_of((i - 1) * tile, tile), 2 * tile, True)

        @pl.when(i == 0)
        def _():
            tile_update(qs, 0, tile, True)

        n_far = jnp.maximum(i - 1, 0)

        def far_body(j, inner):
            tile_update(qs, pl.multiple_of(j * (2 * tile), 2 * tile), 2 * tile, False)
            return inner

        lax.fori_loop(0, n_far // 2, far_body, 0)

        @pl.when(n_far % 2 == 1)
        def _():
            tile_update(qs, pl.multiple_of((n_far - 1) * tile, tile), tile, False)

        for hh in heads:
            c0, c1 = 2 * hh, 2 * hh + 1
            out = (acc_scr[c0, :, :dv] / acc_scr[c0, :, dv:]
                   - lam * (acc_scr[c1, :, :dv] / acc_scr[c1, :, dv:]))
            out = _rmsnorm_rows(out, gain_ref[...]) * (1.0 - lambda_init)
            o_ref[q_rows, hh * dv:(hh + 1) * dv] = out.astype(o_ref.dtype)
        return carry

    lax.fori_loop(0, seq // tile, query_tile, 0)


def _diff_attention(proj, bias_tiles, lam_params, head_gain, *, batch, seq, q_col, k_col, v_col,
                    tile, lambda_init):
    n = batch * seq
    hps = DIFF_HEADS_PER_STEP
    dv = DIFF_V_DIM
    gw = hps * dv
    qb, kb, vb = q_col // gw, k_col // gw, v_col // gw
    maps = 2 * hps
    return pl.pallas_call(
        functools.partial(_diff_kernel, tile=tile, lambda_init=lambda_init),
        grid=(batch, DIFF_HEADS // hps),
        in_specs=[
            pl.BlockSpec((seq, gw), lambda b, g: (b, qb + g)),
            pl.BlockSpec((seq, gw), lambda b, g: (b, kb + g)),
            pl.BlockSpec((seq, gw), lambda b, g: (b, vb + g)),
            pl.BlockSpec((hps, tile, 2 * tile), lambda b, g: (g, 0, 0)),
            pl.BlockSpec(lam_params.shape, lambda b, g: (0, 0)),
            pl.BlockSpec((1, dv), lambda b, g: (0, 0)),
        ],
        out_specs=pl.BlockSpec((seq, gw), lambda b, g: (b, g)),
        out_shape=jax.ShapeDtypeStruct((n, DIFF_HEADS * dv), BF16),
        scratch_shapes=[
            pltpu.VMEM((hps, dv, seq), BF16),
            pltpu.VMEM((hps, seq, dv + LANES), BF16),
            pltpu.VMEM((maps, tile, LANES), F32),
            pltpu.VMEM((maps, tile, dv + LANES), F32),
        ],
        compiler_params=_params("parallel", "parallel"),
        name="diff_attention",
    )(proj, proj, proj, bias_tiles, lam_params, head_gain)


def _ret_log_gamma(h):
    return math.log(1.0 - 2.0 ** (-5.0 - h))


def _retention_kernel(q_ref, k_ref, v_ref, g_ref, qcos_ref, qsin_ref, kcos_ref, ksin_ref,
                      o_ref, state_scr, decay_scr, *, ts):
    @pl.when(pl.program_id(1) == 0)
    def _():
        state_scr[...] = jnp.zeros(state_scr.shape, F32)

    c = RET_CHUNK
    d = RET_DIM
    row = lax.broadcasted_iota(jnp.int32, (c, c), 0).astype(F32)
    col = lax.broadcasted_iota(jnp.int32, (c, c), 1).astype(F32)
    rel = row - col
    heads = range(RET_HEADS)
    col_of = [slice(h * d, (h + 1) * d) for h in heads]
    log_gamma = [_ret_log_gamma(h) for h in heads]
    chunk_decay = [math.exp(c * lg) for lg in log_gamma]

    @pl.when(pl.program_id(1) == 0)
    def _():
        for h, lg in enumerate(log_gamma):
            decay_scr[0, h] = jnp.where(rel >= 0, jnp.exp(jnp.maximum(rel, 0.0) * lg), 0.0)
            decay_scr[1, h] = jnp.exp((c - 1 - row) * lg)
            decay_scr[2, h] = jnp.exp((row + 1) * lg)

    intra, key_decay, query_decay = (decay_scr.at[t] for t in range(3))
    nt_dims = (((1,), (1,)), ((), ()))
    tn_dims = (((0,), (0,)), ((), ()))

    states = [state_scr[h] for h in heads]
    for ci in range(ts // c):
        rows = slice(ci * c, (ci + 1) * c)
        qcos, qsin = qcos_ref[rows, :], qsin_ref[rows, :]
        kcos, ksin = kcos_ref[rows, :], ksin_ref[rows, :]
        qs, ks, kds, vs = [], [], [], []
        for h in heads:
            qh = q_ref[rows, col_of[h]].astype(F32)
            kh = k_ref[rows, col_of[h]].astype(F32)
            qh = qh * qcos + pltpu.roll(qh, d // 2, axis=1) * qsin
            kh = kh * kcos + pltpu.roll(kh, d // 2, axis=1) * ksin
            qs.append(qh.astype(BF16))
            ks.append(kh.astype(BF16))
            kds.append((kh * key_decay[h]).astype(BF16))
            vs.append(v_ref[rows, col_of[h]])
        scores = [lax.dot_general(qs[h], ks[h], nt_dims, preferred_element_type=F32)
                  for h in heads]
        cross = [jnp.dot(qs[h], states[h].astype(BF16), preferred_element_type=F32)
                 for h in heads]
        kv = [lax.dot_general(kds[h], vs[h], tn_dims, preferred_element_type=F32)
              for h in heads]
        inner = [jnp.dot((scores[h] * intra[h]).astype(BF16), vs[h], preferred_element_type=F32)
                 for h in heads]
        for h in heads:
            states[h] = chunk_decay[h] * states[h] + kv[h]
            out = _rmsnorm_rows(inner[h] + cross[h] * query_decay[h])
            gate = g_ref[rows, col_of[h]].astype(F32)
            o_ref[rows, col_of[h]] = (gate * jax.nn.sigmoid(gate) * out).astype(o_ref.dtype)
    for h in heads:
        state_scr[h] = states[h]


def _rotary_tables(seq):
    half = RET_DIM // 2
    inv_freq = 1.0 / (ROPE_BASE ** jnp.linspace(0.0, 1.0, half, dtype=F32))
    ang = jnp.arange(seq, dtype=F32)[:, None] * inv_freq[None, :]
    cos, sin = jnp.cos(ang), jnp.sin(ang)
    cos2 = jnp.concatenate([cos, cos], axis=-1)
    sin2 = jnp.concatenate([-sin, sin], axis=-1)
    kscale = RET_DIM ** -0.5
    return cos2, sin2, cos2 * kscale, sin2 * kscale


def _retention(proj, tables, *, batch, seq, mix, q_col, ts):
    n = batch * seq
    spb = seq // ts
    cb = q_col // mix
    act = lambda off: pl.BlockSpec((ts, mix), lambda b, s: (b * spb + s, cb + off))
    tab = pl.BlockSpec((ts, RET_DIM), lambda b, s: (s, 0))
    return pl.pallas_call(
        functools.partial(_retention_kernel, ts=ts),
        grid=(batch, spb),
        in_specs=[act(0), act(1), act(2), act(3), tab, tab, tab, tab],
        out_specs=pl.BlockSpec((ts, mix), lambda b, s: (b * spb + s, 0)),
        out_shape=jax.ShapeDtypeStruct((n, mix), BF16),
        scratch_shapes=[pltpu.VMEM((RET_HEADS, RET_DIM, RET_DIM), F32),
                        pltpu.VMEM((3, RET_HEADS, RET_CHUNK, RET_DIM), F32)],
        compiler_params=_params("parallel", "arbitrary"),
        name="retention",
    )(proj, proj, proj, proj, *tables)


def _merge_kernel(yp_ref, yd_ref, yr_ref, g0_ref, g1_ref, g2_ref, w_ref, o_ref):
    acc = None
    for n, (y_ref, g_ref) in enumerate(((yp_ref, g0_ref), (yd_ref, g1_ref), (yr_ref, g2_ref))):
        wide = jnp.dot(y_ref[...], w_ref[n], preferred_element_type=F32)
        term = jax.nn.sigmoid(g_ref[...].astype(F32)) * wide
        acc = term if acc is None else acc + term
    o_ref[...] = acc.astype(o_ref.dtype)


def _merge(y_pool, y_diff, y_ret, proj, w_branch, layer, *, gate_col, tm, tn):
    n, mix = y_pool.shape
    d = w_branch.shape[3]
    y_spec = pl.BlockSpec((tm, mix), lambda i, j: (i, 0))
    gate = lambda br: pl.BlockSpec((tm, tn), lambda i, j: (i, (gate_col + br * d) // tn + j))
    return pl.pallas_call(
        _merge_kernel,
        grid=(n // tm, d // tn),
        in_specs=[y_spec, y_spec, y_spec, gate(0), gate(1), gate(2),
                  pl.BlockSpec((None, N_BRANCHES, mix, tn), lambda i, j: (layer, 0, 0, j))],
        out_specs=pl.BlockSpec((tm, tn), lambda i, j: (i, j)),
        out_shape=jax.ShapeDtypeStruct((n, d), BF16),
        compiler_params=_params("parallel", "arbitrary"),
        name="merge",
    )(y_pool, y_diff, y_ret, proj, proj, proj, w_branch)


def _outproj_kernel(m_ref, w_ref, x_ref, g_ref, o_ref):
    mixed = jnp.dot(m_ref[...], w_ref[...], preferred_element_type=F32)
    o_ref[...] = x_ref[...] + _rmsnorm_rows(mixed, g_ref[...])


def _outproj(merged, w_out, x, gain, layer, *, tm):
    n, d = x.shape
    row = pl.BlockSpec((tm, d), lambda i: (i, 0))
    return pl.pallas_call(
        _outproj_kernel,
        grid=(n // tm,),
        in_specs=[row, pl.BlockSpec((None, d, d), lambda i: (layer, 0, 0)), row,
                  pl.BlockSpec((1, d), lambda i: (0, 0))],
        out_specs=row,
        out_shape=jax.ShapeDtypeStruct((n, d), F32),
        compiler_params=_params("parallel"),
        name="outproj",
    )(merged, w_out, x, gain)


def _mlp_kernel(x_ref, gpre_ref, wup_ref, wdown_ref, gpost_ref, o_ref, h_scr):
    acc_scr = o_ref
    k = pl.program_id(1)
    last = pl.num_programs(1) - 1
    chunks = _row_chunks(x_ref.shape[0])

    def ff_partial(rows):
        up = jnp.dot(h_scr[rows], wup_ref[...], preferred_element_type=F32)
        ff = jnp.square(jnp.maximum(up, 0.0)).astype(BF16)
        return jnp.dot(ff, wdown_ref[...], preferred_element_type=F32)

    @pl.when(k == 0)
    def _():
        for rows in chunks:
            h_scr[rows] = _rmsnorm_rows(x_ref[rows], gpre_ref[...]).astype(BF16)
            acc_scr[rows] = ff_partial(rows)

    @pl.when(jnp.logical_and(k > 0, k < last))
    def _():
        acc_scr[...] += ff_partial(slice(None))

    @pl.when(k == last)
    def _():
        for rows in chunks:
            ff = acc_scr[rows] + ff_partial(rows)
            o_ref[rows] = x_ref[rows] + _rmsnorm_rows(ff, gpost_ref[...])


def _mlp(x, gain_pre, w_up, w_down, gain_post, layer, *, tm, tf):
    n, d = x.shape
    f = w_up.shape[2]
    assert f // tf >= 2
    row = pl.BlockSpec((tm, d), lambda i, k: (i, 0))
    vec = pl.BlockSpec((1, d), lambda i, k: (0, 0))
    return pl.pallas_call(
        _mlp_kernel,
        grid=(n // tm, f // tf),
        in_specs=[row, vec, pl.BlockSpec((None, d, tf), lambda i, k: (layer, 0, k)),
                  pl.BlockSpec((None, tf, d), lambda i, k: (layer, k, 0)), vec],
        out_specs=row,
        out_shape=jax.ShapeDtypeStruct((n, d), F32),
        scratch_shapes=[pltpu.VMEM((tm, d), BF16)],
        compiler_params=_params("parallel", "arbitrary"),
        name="mlp",
    )(x, gain_pre, w_up, w_down, gain_post)


def _tiles(n_tokens, seq, in_cols, d_model, d_ff):
    pick = lambda total, want: want if total % want == 0 else total
    return dict(
        inproj=dict(tm=pick(n_tokens, 1024), tn=pick(in_cols, 2048)),
        pool=dict(ts=pick(seq, 512)),
        diff=dict(tile=pick(seq, 512)),
        ret=dict(ts=pick(seq, 512)),
        merge=dict(tm=pick(n_tokens, 512), tn=d_model),
        outproj=dict(tm=pick(n_tokens, 512)),
        mlp=dict(tm=pick(n_tokens, 512), tf=pick(d_ff, 2048)),
    )


def kernel(x, rel_bias, norm_pre_mix, norm_post_mix, norm_pre_mlp, norm_post_mlp, w_in, pool_w,
           pool_scale, diff_lambda, diff_head_norm, w_branch, w_out, w_up, w_down):
    batch, seq, d_model = x.shape
    depth, _, in_cols = w_in.shape
    mix = pool_scale.shape[1]
    d_ff = w_up.shape[2]
    n = batch * seq
    assert mix == DIFF_HEADS * DIFF_V_DIM == RET_HEADS * RET_DIM
    assert in_cols == 8 * mix + N_BRANCHES * d_model
    tiles = _tiles(n, seq, in_cols, d_model, d_ff)

    col = lambda idx: idx * mix
    bias_tiles = _bias_tiles(rel_bias, tile=tiles["diff"]["tile"])
    tables = _rotary_tables(seq)
    vec = lambda a: a.astype(F32).reshape(1, -1)
    w_in, pool_w, w_branch, w_out, w_up, w_down = (
        a.astype(BF16) for a in (w_in, pool_w, w_branch, w_out, w_up, w_down))

    xs = x.astype(F32).reshape(n, d_model)
    for l in range(depth):
        lambda_init = 0.8 - 0.6 * math.exp(-0.3 * l)
        proj = _inproj(xs, vec(norm_pre_mix[l]), w_in, l, **tiles["inproj"])
        y_pool = _pool(proj, pool_w, vec(pool_scale[l]), l,
                       batch=batch, seq=seq, mix=mix, **tiles["pool"])
        y_diff = _diff_attention(proj, bias_tiles, diff_lambda[l].astype(F32),
                                 vec(diff_head_norm[l]), batch=batch, seq=seq,
                                 q_col=col(1), k_col=col(2), v_col=col(3),
                                 lambda_init=lambda_init, **tiles["diff"])
        y_ret = _retention(proj, tables, batch=batch, seq=seq, mix=mix, q_col=col(4),
                           **tiles["ret"])
        merged = _merge(y_pool, y_diff, y_ret, proj, w_branch, l,
                        gate_col=col(8), **tiles["merge"])
        xs = _outproj(merged, w_out, xs, vec(norm_post_mix[l]), l, **tiles["outproj"])
        xs = _mlp(xs, vec(norm_pre_mlp[l]), w_up, w_down, vec(norm_post_mlp[l]), l,
                  **tiles["mlp"])
    return xs.reshape(batch, seq, d_model).astype(x.dtype)
```
